```python
import jax
import jax.numpy as jnp
from jax import lax
import numpy as np

D_MODEL = 1024
BATCH = 8
SEQ = 2048
DEPTH = 2

HEAD_DIM = 64
ROPE_THETA = 10000.0
RMS_EPS = 1e-6
NEG_INF = -1e30
FORCE_SCORE = 1e9
BAND_BLK = 128

DIL_GROUPS = ((128, 1), (512, 4), (2048, 16))
A_GROUPS = len(DIL_GROUPS)
A_HEADS = 4
A_WIDTH = A_GROUPS * A_HEADS * HEAD_DIM
A_OUT = A_HEADS * HEAD_DIM

NSA_HEADS = 8
NSA_KV = 2
NSA_REP = NSA_HEADS // NSA_KV
NSA_Q_WIDTH = NSA_HEADS * HEAD_DIM
NSA_KV_WIDTH = NSA_KV * HEAD_DIM
NSA_BRANCHES = 3
CMP_LEN = 32
CMP_STRIDE = 16
CMP_HID = 256
SEL_LEN = 64
SEL_TOPK = 16
SEL_QBLK = 64
WIN_LEN = 512

D_FF = 2816
CONV_W = 3
N_MOD = 6

IN_SIZES = (A_WIDTH, A_WIDTH, A_WIDTH, NSA_Q_WIDTH,
            NSA_KV_WIDTH, NSA_KV_WIDTH, NSA_KV_WIDTH, NSA_KV_WIDTH, NSA_KV_WIDTH, NSA_KV_WIDTH,
            NSA_HEADS * NSA_BRANCHES, D_MODEL, D_MODEL)
N_IN = sum(IN_SIZES)

kernel_name = 'hybrid_dilated_nsa_convffn_adaln_block'


def rms_norm(x, g):
    xf = x.astype(jnp.float32)
    y = xf * lax.rsqrt(jnp.mean(xf * xf, axis=-1, keepdims=True) + RMS_EPS)
    return (y * g.astype(jnp.float32)).astype(x.dtype)


def rope_tables(seq):
    inv = 1.0 / (ROPE_THETA ** (jnp.arange(0, HEAD_DIM, 2, dtype=jnp.float32) / HEAD_DIM))
    ang = jnp.arange(seq, dtype=jnp.float32)[:, None] * inv[None, :]
    return jnp.cos(ang), jnp.sin(ang)


def apply_rope(x, cos, sin):
    x1, x2 = jnp.split(x, 2, axis=-1)
    c = cos[None, :, None, :].astype(x.dtype)
    s = sin[None, :, None, :].astype(x.dtype)
    return jnp.concatenate([x1 * c - x2 * s, x2 * c + x1 * s], axis=-1)


def banded_attention(q, k, v, n_back):
    n, L, G, R, hd = q.shape
    nb = -(-L // BAND_BLK)
    n_prev = -(-n_back // BAND_BLK)
    pad = nb * BAND_BLK - L
    kw_len = (n_prev + 1) * BAND_BLK
    qb = jnp.pad(q, ((0, 0), (0, pad), (0, 0), (0, 0), (0, 0))).reshape(n, nb, BAND_BLK, G, R, hd)
    kv_pad = ((0, 0), (n_prev * BAND_BLK, pad), (0, 0), (0, 0))
    win = jnp.arange(nb)[:, None] + jnp.arange(n_prev + 1)[None, :]
    kb = jnp.pad(k, kv_pad).reshape(n, nb + n_prev, BAND_BLK, G, hd)[:, win].reshape(n, nb, kw_len, G, hd)
    vb = jnp.pad(v, kv_pad).reshape(n, nb + n_prev, BAND_BLK, G, hd)[:, win].reshape(n, nb, kw_len, G, hd)
    s = jnp.einsum('nbqgrd,nbkgd->nbgrqk', qb, kb).astype(jnp.float32) * (hd ** -0.5)
    qpos = jnp.arange(nb)[:, None] * BAND_BLK + jnp.arange(BAND_BLK)[None, :]
    kpos = (jnp.arange(nb)[:, None] - n_prev) * BAND_BLK + jnp.arange(kw_len)[None, :]
    dist = qpos[:, :, None] - kpos[:, None, :]
    mask = (dist >= 0) & (dist <= n_back) & (kpos[:, None, :] >= 0)
    s = jnp.where(mask[None, :, None, None], s, NEG_INF)
    m = jnp.max(s, axis=-1, keepdims=True)
    e = jnp.exp(s - m)
    den = jnp.sum(e, axis=-1, keepdims=True)
    o = jnp.einsum('nbgrqk,nbkgd->nbqgrd', (e / den).astype(v.dtype), vb)
    o = o.reshape(n, nb * BAND_BLK, G, R, hd)[:, :L]
    lse = (m + jnp.log(den))[..., 0].transpose(0, 1, 4, 2, 3).reshape(n, nb * BAND_BLK, G, R)[:, :L]
    return o, lse


def to_strided(t, dil):
    B, S = t.shape[:2]
    rest = t.shape[2:]
    return jnp.swapaxes(t.reshape(B, S // dil, dil, *rest), 1, 2).reshape(B * dil, S // dil, *rest)


def from_strided(t, B, dil):
    n, L = t.shape[:2]
    rest = t.shape[2:]
    return jnp.swapaxes(t.reshape(B, dil, L, *rest), 1, 2).reshape(B, L * dil, *rest)


def dilated_attention(q, k, v, cos, sin):
    B, S = q.shape[:2]
    flat = (B, S, A_GROUPS * A_HEADS, HEAD_DIM)
    q = apply_rope(q.reshape(flat), cos, sin).reshape(q.shape)
    k = apply_rope(k.reshape(flat), cos, sin).reshape(k.shape)
    outs, lses = [], []
    for g, (window, dil) in enumerate(DIL_GROUPS):
        o, lse = banded_attention(to_strided(q[:, :, g], dil)[:, :, :, None],
                                  to_strided(k[:, :, g], dil), to_strided(v[:, :, g], dil), window // dil)
        outs.append(from_strided(o[:, :, :, 0], B, dil))
        lses.append(from_strided(lse[:, :, :, 0], B, dil))
    w = jax.nn.softmax(jnp.stack(lses, axis=0), axis=0)
    out = jnp.einsum('gbsh,gbshd->bshd', w.astype(v.dtype), jnp.stack(outs, axis=0))
    return out.reshape(B, S, A_OUT)


def nsa_attention(q, k_cmp, v_cmp, k_sel, v_sel, k_win, v_win, gate_logits,
                  pe_k, pe_v, w1_k, w2_k, w1_v, w2_v, cos, sin):
    B, S = q.shape[:2]
    scale = HEAD_DIM ** -0.5
    qh = q.reshape(B, S, NSA_HEADS, HEAD_DIM)
    q_rot = apply_rope(qh, cos, sin).reshape(B, S, NSA_KV, NSA_REP, HEAD_DIM)
    q_nope = qh.reshape(B, S, NSA_KV, NSA_REP, HEAD_DIM)
    tpos = jnp.arange(S)

    n_cmp = (S - CMP_LEN) // CMP_STRIDE + 1
    starts = jnp.arange(n_cmp) * CMP_STRIDE
    blk_idx = starts[:, None] + jnp.arange(CMP_LEN)[None, :]

    def compress(t, pe, w1, w2):
        blocks = t[:, blk_idx] + pe[:, None, :]
        flat = blocks.transpose(0, 1, 3, 2, 4).reshape(B, n_cmp, NSA_KV, CMP_LEN * HEAD_DIM)
        return jax.nn.silu(flat @ w1) @ w2

    kc = compress(k_cmp, pe_k, w1_k, w2_k)
    vc = compress(v_cmp, pe_v, w1_v, w2_v)
    s_cmp = jnp.einsum('bsgrd,bngd->bsgrn', q_nope, kc).astype(jnp.float32) * scale
    cmask = ((starts + CMP_LEN - 1)[None, :] <= tpos[:, None])[None, :, None, None, :]
    p_cmp = jnp.where(cmask, jax.nn.softmax(jnp.where(cmask, s_cmp, NEG_INF), axis=-1), 0.0)
    o_cmp = jnp.einsum('bsgrn,bngd->bsgrd', p_cmp.astype(vc.dtype), vc)

    n_blk = S // SEL_LEN
    bstart = jnp.arange(n_blk) * SEL_LEN
    overlap = ((starts[:, None] < bstart[None, :] + SEL_LEN) &
               (starts[:, None] + CMP_LEN > bstart[None, :])).astype(jnp.float32)
    imp = jnp.einsum('bsgrn,nj->bsgj', p_cmp, overlap)
    cur = tpos // SEL_LEN
    jb = jnp.arange(n_blk)
    forced = (jb[None, :] == 0) | (jb[None, :] == cur[:, None]) | (jb[None, :] == cur[:, None] - 1)
    valid = bstart[None, :] <= tpos[:, None]
    imp = jnp.where(forced[None, :, None, :], FORCE_SCORE, imp)
    imp = jnp.where(valid[None, :, None, :], imp, NEG_INF)
    n_top = min(SEL_TOPK, n_blk)
    _, sel_idx = lax.top_k(imp, n_top)

    kb = apply_rope(k_sel, cos, sin).reshape(B, n_blk, SEL_LEN, NSA_KV, HEAD_DIM).transpose(0, 3, 1, 2, 4)
    vb = v_sel.reshape(B, n_blk, SEL_LEN, NSA_KV, HEAD_DIM).transpose(0, 3, 1, 2, 4)
    nq = S // SEL_QBLK
    bi = jnp.arange(B)[:, None, None, None]
    gi = jnp.arange(NSA_KV)[None, None, :, None]

    def chunks(t):
        return jnp.swapaxes(t.reshape(B, nq, SEL_QBLK, *t.shape[2:]), 0, 1)

    def sel_block(args):
        qc, ic, ci = args
        kg = kb[bi, gi, ic]
        vg = vb[bi, gi, ic]
        s = jnp.einsum('bcgrd,bcgjld->bcgrjl', qc, kg).astype(jnp.float32) * scale
        qp = ci * SEL_QBLK + jnp.arange(SEL_QBLK)
        kp = ic[..., None] * SEL_LEN + jnp.arange(SEL_LEN)
        mask = (kp <= qp[None, :, None, None, None])[:, :, :, None]
        s = jnp.where(mask, s, NEG_INF)
        p = jax.nn.softmax(s.reshape(*s.shape[:4], -1), axis=-1).reshape(s.shape)
        return jnp.einsum('bcgrjl,bcgjld->bcgrd', p.astype(vg.dtype), vg)

    o_sel = lax.map(sel_block, (chunks(q_rot), chunks(sel_idx), jnp.arange(nq)))
    o_sel = jnp.swapaxes(o_sel, 0, 1).reshape(B, S, NSA_KV, NSA_REP, HEAD_DIM)

    o_win, _ = banded_attention(q_rot, apply_rope(k_win, cos, sin), v_win, WIN_LEN - 1)

    g = jax.nn.sigmoid(gate_logits.astype(jnp.float32)).reshape(B, S, NSA_KV, NSA_REP, NSA_BRANCHES)
    g = g.astype(q.dtype)
    out = g[..., 0:1] * o_cmp + g[..., 1:2] * o_sel + g[..., 2:3] * o_win
    return out.reshape(B, S, NSA_Q_WIDTH)


def causal_depthwise_conv(u, w, b):
    y = lax.conv_general_dilated(u, w[:, None, :], window_strides=(1,), padding=((CONV_W - 1, 0),),
                                 dimension_numbers=('NWC', 'WIO', 'NWC'), feature_group_count=u.shape[-1])
    return y + b


def setup_inputs(seed: int = 0) -> dict:
    key = jax.random.key(seed)
    ks = jax.random.split(key, 24)
    f32 = jnp.float32
    D = D_MODEL

    def nrm(k, shape, std):
        return jax.random.normal(k, shape, f32) * std

    return {
        'x': nrm(ks[0], (BATCH, SEQ, D), 1.0),
        'c': nrm(ks[1], (BATCH, D), 1.0),
        'norm1_g': 1.0 + nrm(ks[2], (DEPTH, D), 0.02),
        'norm2_g': 1.0 + nrm(ks[3], (DEPTH, D), 0.02),
        'final_g': 1.0 + nrm(ks[4], (D,), 0.02),
        'w_mod': nrm(ks[5], (DEPTH, D, N_MOD * D), 0.5 * D ** -0.5),
        'b_mod': nrm(ks[6], (DEPTH, N_MOD * D), 0.01),
        'w_in': nrm(ks[7], (DEPTH, D, N_IN), D ** -0.5),
        'cmp_pe_k': nrm(ks[8], (DEPTH, CMP_LEN, HEAD_DIM), 0.1),
        'cmp_pe_v': nrm(ks[9], (DEPTH, CMP_LEN, HEAD_DIM), 0.1),
        'cmp_w1_k': nrm(ks[10], (DEPTH, CMP_LEN * HEAD_DIM, CMP_HID), (CMP_LEN * HEAD_DIM) ** -0.5),
        'cmp_w2_k': nrm(ks[11], (DEPTH, CMP_HID, HEAD_DIM), CMP_HID ** -0.5),
        'cmp_w1_v': nrm(ks[12], (DEPTH, CMP_LEN * HEAD_DIM, CMP_HID), (CMP_LEN * HEAD_DIM) ** -0.5),
        'cmp_w2_v': nrm(ks[13], (DEPTH, CMP_HID, HEAD_DIM), CMP_HID ** -0.5),
        'w_br_a': nrm(ks[14], (DEPTH, A_OUT, D), A_OUT ** -0.5),
        'w_br_b': nrm(ks[15], (DEPTH, NSA_Q_WIDTH, D), NSA_Q_WIDTH ** -0.5),
        'w_out': nrm(ks[16], (DEPTH, D, D), D ** -0.5),
        'w_up': nrm(ks[17], (DEPTH, D, 2 * D_FF), D ** -0.5),
        'conv_w': nrm(ks[18], (DEPTH, CONV_W, 2 * D_FF), 0.5),
        'conv_b': nrm(ks[19], (DEPTH, 2 * D_FF), 0.01),
        'w_down': nrm(ks[20], (DEPTH, D_FF, D), D_FF ** -0.5),
    }


def reference(x, c, norm1_g, norm2_g, final_g, w_mod, b_mod, w_in, cmp_pe_k, cmp_pe_v,
              cmp_w1_k, cmp_w2_k, cmp_w1_v, cmp_w2_v, w_br_a, w_br_b, w_out, w_up,
              conv_w, conv_b, w_down):
    B, S, D = x.shape
    cos, sin = rope_tables(S)
    split_at = np.cumsum(IN_SIZES)[:-1].tolist()
    a_shape = (B, S, A_GROUPS, A_HEADS, HEAD_DIM)
    kv_shape = (B, S, NSA_KV, HEAD_DIM)
    for layer in range(DEPTH):
        mod = jax.nn.silu(c) @ w_mod[layer] + b_mod[layer]
        shift1, scale1, gate1, shift2, scale2, gate2 = jnp.split(mod[:, None, :], N_MOD, axis=-1)

        h = rms_norm(x, norm1_g[layer]) * (1 + scale1) + shift1
        (aq, ak, av, bq, kc, vc, ksl, vsl, kwn, vwn, blog, ga, gb) = jnp.split(
            h @ w_in[layer], split_at, axis=-1)
        y_a = dilated_attention(aq.reshape(a_shape), ak.reshape(a_shape), av.reshape(a_shape), cos, sin)
        y_b = nsa_attention(bq, kc.reshape(kv_shape), vc.reshape(kv_shape), ksl.reshape(kv_shape),
                            vsl.reshape(kv_shape), kwn.reshape(kv_shape), vwn.reshape(kv_shape), blog,
                            cmp_pe_k[layer], cmp_pe_v[layer], cmp_w1_k[layer], cmp_w2_k[layer],
                            cmp_w1_v[layer], cmp_w2_v[layer], cos, sin)
        merged = jax.nn.sigmoid(ga) * (y_a @ w_br_a[layer]) + jax.nn.sigmoid(gb) * (y_b @ w_br_b[layer])
        x = x + gate1 * (merged @ w_out[layer])

        h = rms_norm(x, norm2_g[layer]) * (1 + scale2) + shift2
        u = causal_depthwise_conv(h @ w_up[layer], conv_w[layer], conv_b[layer])
        u_gate, u_val = jnp.split(u, 2, axis=-1)
        x = x + gate2 * ((jax.nn.silu(u_gate) * u_val) @ w_down[layer])
    return rms_norm(x, final_g)
```

```python
import functools

import jax
import jax.numpy as jnp
import numpy as np
from jax import lax
from jax.experimental import pallas as pl
from jax.experimental.pallas import tpu as pltpu

F32 = jnp.float32
BF16 = jnp.bfloat16

HEAD_DIM = 64
ROPE_THETA = 10000.0
RMS_EPS = 1e-6
NEG_INF = -1e30
FORCE_SCORE = 1e9
MASK_BIG = float(2.0 ** 100)

DIL_GROUPS = ((128, 1), (512, 4), (2048, 16))
A_HEADS = 4
A_GROUP_WIDTH = A_HEADS * HEAD_DIM
BAND_BACK = 128
NSA_HEADS = 8
NSA_KV = 2
NSA_REP = NSA_HEADS // NSA_KV
NSA_Q_WIDTH = NSA_HEADS * HEAD_DIM
CMP_LEN = 32
CMP_STRIDE = 16
CMP_HID = 256
SEL_LEN = 64
SEL_TOPK = 16
WIN_LEN = 512
D_FF = 2816
CONV_W = 3
N_MOD = 6

LANES = 128
VMEM_LIMIT = 56 * 1024 * 1024

ROW_TILE = 512
NSA_TILE = 256
FFN_CHUNK = 256
HALO = 8


def _sigmoid(x):
    return 1.0 / (1.0 + jnp.exp(-x))


def _silu(x):
    return x * _sigmoid(x)


def _params(semantics):
    return pltpu.CompilerParams(dimension_semantics=semantics, vmem_limit_bytes=VMEM_LIMIT)


def _resident(shape):
    nd = len(shape)
    return pl.BlockSpec(tuple(shape), lambda *_: (0,) * nd, pipeline_mode=pl.Buffered(1))


def _lane_iota(shape):
    return lax.broadcasted_iota(jnp.int32, shape, len(shape) - 1)


def _row_iota(shape):
    return lax.broadcasted_iota(jnp.int32, shape, len(shape) - 2)


def _mod_kernel(c_ref, w_ref, b_ref, o_ref):
    sc = _silu(c_ref[...])
    o_ref[0] = jnp.dot(sc, w_ref[0], preferred_element_type=F32) + b_ref[0]


def _modulation(c, w_mod, b_mod):
    depth, d, n = w_mod.shape
    batch = c.shape[0]
    tn = 1024
    return pl.pallas_call(
        _mod_kernel,
        out_shape=jax.ShapeDtypeStruct((depth, batch, n), F32),
        grid=(depth, n // tn),
        in_specs=[
            pl.BlockSpec((batch, d), lambda l, j: (0, 0)),
            pl.BlockSpec((1, d, tn), lambda l, j: (l, 0, j)),
            pl.BlockSpec((1, 1, tn), lambda l, j: (l, 0, j)),
        ],
        out_specs=pl.BlockSpec((1, batch, tn), lambda l, j: (l, 0, j)),
        compiler_params=_params(("arbitrary", "arbitrary")),
        name="modulation",
    )(c, w_mod, b_mod.reshape(depth, 1, n))


def _norm_mod(x, g, scale, shift):
    ms = jnp.mean(x * x, axis=-1, keepdims=True)
    y = x * lax.rsqrt(ms + RMS_EPS) * g
    return y * (1.0 + scale) + shift


def _rope_chunk(xc, cos, sin_signed, first_half):
    partner = jnp.where(first_half, pltpu.roll(xc, LANES - HEAD_DIM // 2, 1),
                        pltpu.roll(xc, HEAD_DIM // 2, 1))
    return xc * cos + partner * sin_signed


_QKV_A = 3 * A_GROUP_WIDTH
_OFF_BQ = 3 * _QKV_A
_OFF_KV = _OFF_BQ + NSA_Q_WIDTH
_OFF_BLOG = _OFF_KV + 6 * LANES
_OFF_GA = _OFF_BLOG + LANES
N_IN_PACKED = _OFF_GA + 2 * 1024


def _inproj_kernel(x_ref, shift_ref, scale_ref, g_ref, cos_ref, sin_ref, w_ref,
                   qkv0_ref, qkv1_ref, qkv2_ref, bqr_ref, bqn_ref, kc_ref, vc_ref,
                   ksl_ref, vsl_ref, kwn_ref, vwn_ref, blog_ref, ga_ref, gb_ref):
    h = _norm_mod(x_ref[0], g_ref[...], scale_ref[...], shift_ref[...]).astype(BF16)
    cos = cos_ref[...]
    sin = sin_ref[...]
    rows = cos.shape[0]
    first_half = (_lane_iota((rows, LANES)) % HEAD_DIM) < (HEAD_DIM // 2)
    scale = HEAD_DIM ** -0.5

    def proj(off, width):
        return jnp.dot(h, w_ref[:, off:off + width], preferred_element_type=F32)

    def rope(y, mult):
        parts = [_rope_chunk(y[:, c:c + LANES], cos, sin, first_half) * mult
                 for c in range(0, y.shape[1], LANES)]
        return parts[0] if len(parts) == 1 else jnp.concatenate(parts, axis=1)

    for g, out_ref in enumerate((qkv0_ref, qkv1_ref, qkv2_ref)):
        base = g * _QKV_A
        out_ref[0, :, 0:256] = rope(proj(base, 256), scale).astype(out_ref.dtype)
        out_ref[0, :, 256:512] = rope(proj(base + 256, 256), 1.0).astype(out_ref.dtype)
        out_ref[0, :, 512:768] = proj(base + 512, 256).astype(out_ref.dtype)

    bq = proj(_OFF_BQ, NSA_Q_WIDTH)
    bqn_ref[0] = (bq * scale).astype(bqn_ref.dtype)
    bqr_ref[0] = rope(bq, scale).astype(bqr_ref.dtype)

    kv = proj(_OFF_KV, 6 * LANES)
    kc_ref[0] = kv[:, 0:128].astype(kc_ref.dtype)
    vc_ref[0] = kv[:, 128:256].astype(vc_ref.dtype)
    ksl_ref[0] = rope(kv[:, 256:384], 1.0).astype(ksl_ref.dtype)
    vsl_ref[0] = kv[:, 384:512].astype(vsl_ref.dtype)
    kwn_ref[0] = rope(kv[:, 512:640], 1.0).astype(kwn_ref.dtype)
    vwn_ref[0] = kv[:, 640:768].astype(vwn_ref.dtype)

    blog_ref[0] = proj(_OFF_BLOG, LANES)
    ga_ref[0] = proj(_OFF_GA, 1024).astype(ga_ref.dtype)
    gb_ref[0] = proj(_OFF_GA + 1024, 1024).astype(gb_ref.dtype)


def _pack_w_in(w):
    aq, ak, av = w[:, 0:768], w[:, 768:1536], w[:, 1536:2304]
    cols = []
    for g in range(3):
        sl = slice(g * 256, (g + 1) * 256)
        cols += [aq[:, sl], ak[:, sl], av[:, sl]]
    cols.append(w[:, 2304:3584])
    cols.append(jnp.pad(w[:, 3584:3608], ((0, 0), (0, LANES - 24))))
    cols.append(w[:, 3608:5656])
    return jnp.concatenate(cols, axis=1).astype(BF16)


def _input_projection(x, mod_l, norm_g, cos_t, sin_t, w_packed):
    batch, seq, d = x.shape
    tm = ROW_TILE
    row = lambda w: pl.BlockSpec((1, tm, w), lambda b, i: (b, i, 0))
    vec = lambda k: pl.BlockSpec((None, None, 1, d), lambda b, i, k=k: (b, k, 0, 0))
    shp = lambda w, dt: jax.ShapeDtypeStruct((batch, seq, w), dt)
    return pl.pallas_call(
        _inproj_kernel,
        out_shape=[shp(768, BF16)] * 3 + [shp(512, BF16)] * 2 + [shp(128, BF16)] * 2
        + [shp(128, F32)] * 4 + [shp(128, F32)] + [shp(1024, BF16)] * 2,
        grid=(batch, seq // tm),
        in_specs=[
            row(d), vec(0), vec(1),
            pl.BlockSpec((1, d), lambda b, i: (0, 0)),
            pl.BlockSpec((tm, LANES), lambda b, i: (i, 0)),
            pl.BlockSpec((tm, LANES), lambda b, i: (i, 0)),
            _resident((d, N_IN_PACKED)),
        ],
        out_specs=[row(768)] * 3 + [row(512)] * 2 + [row(128)] * 7 + [row(1024)] * 2,
        compiler_params=_params(("arbitrary", "arbitrary")),
        name="input_projection",
    )(x, mod_l, mod_l, norm_g.reshape(1, d), cos_t, sin_t, w_packed)


def _banded_kernel(qkv_ref, o_ref, lse_ref, *, length, nres):
    blk = 128
    nkeys = 2 * blk if length > blk else blk
    lane = _lane_iota((blk, LANES))
    lo = lane < HEAD_DIM
    qrow = _row_iota((2 * blk, nkeys)) % blk
    kcol = _lane_iota((2 * blk, nkeys))

    def unit(i, res, pair):
        base = res * _QKV_A + pair * LANES
        r0 = pl.multiple_of(i * blk, blk)
        qp = qkv_ref[0, pl.ds(r0, blk), base:base + LANES]
        if length > blk:
            ks = pl.multiple_of(jnp.maximum(r0 - blk, 0), blk)
        else:
            ks = 0
        k2 = qkv_ref[0, pl.ds(ks, nkeys), base + 256:base + 256 + LANES]
        v2 = qkv_ref[0, pl.ds(ks, nkeys), base + 512:base + 512 + LANES]
        zero = jnp.zeros_like(qp)
        qs = jnp.concatenate([jnp.where(lo, qp, zero), jnp.where(lo, zero, qp)], axis=0)
        s = lax.dot_general(qs, k2, (((1,), (1,)), ((), ())), preferred_element_type=F32)
        dist = (r0 + qrow) - (ks + kcol)
        s = jnp.where((dist >= 0) & (dist <= BAND_BACK), s, NEG_INF)
        m = jnp.max(s, axis=-1, keepdims=True)
        e = jnp.exp(s - m)
        den = jnp.sum(e, axis=-1, keepdims=True)
        o = jnp.dot(e.astype(BF16), v2, preferred_element_type=F32) * (1.0 / den)
        lse = m + jnp.log(den)
        ob = res * A_GROUP_WIDTH + pair * LANES
        o_ref[0, pl.ds(r0, blk), ob:ob + LANES] = jnp.where(lo, o[:blk], o[blk:]).astype(o_ref.dtype)
        lse_ref[0, pl.ds(r0, blk), ob:ob + LANES] = jnp.where(
            lo, jnp.broadcast_to(lse[:blk], (blk, LANES)), jnp.broadcast_to(lse[blk:], (blk, LANES)))

    def body(i, carry):
        for res in range(nres):
            for pair in range(2):
                unit(i, res, pair)
        return carry

    lax.fori_loop(0, length // blk, body, 0)


def _banded_attention(qkv, dil):
    batch, seq, _ = qkv.shape
    length = seq // dil
    nres = min(dil, 4)
    view = qkv.reshape(batch, length, dil * _QKV_A)
    o, lse = pl.pallas_call(
        functools.partial(_banded_kernel, length=length, nres=nres),
        out_shape=[jax.ShapeDtypeStruct((batch, length, dil * A_GROUP_WIDTH), BF16),
                   jax.ShapeDtypeStruct((batch, length, dil * A_GROUP_WIDTH), F32)],
        grid=(batch, dil // nres),
        in_specs=[pl.BlockSpec((1, length, nres * _QKV_A), lambda b, r: (b, 0, r))],
        out_specs=[pl.BlockSpec((1, length, nres * A_GROUP_WIDTH), lambda b, r: (b, 0, r))] * 2,
        compiler_params=_params(("arbitrary", "arbitrary")),
        name=f"banded_attention_d{dil}",
    )(view)
    return o.reshape(batch, seq, A_GROUP_WIDTH), lse.reshape(batch, seq, A_GROUP_WIDTH)


def _compress_kernel(k_ref, v_ref, pe_ref, w1_ref, w2_ref, kc_ref, vc_ref):
    nchunk = k_ref.shape[1]
    for t, (src, dst) in enumerate(((k_ref, kc_ref), (v_ref, vc_ref))):
        c = src[0].astype(F32)
        top = jnp.dot((c + pe_ref[t, 0]).astype(BF16), w1_ref[t, 0], preferred_element_type=F32)
        bot = jnp.dot((c + pe_ref[t, 1]).astype(BF16), w1_ref[t, 1], preferred_element_type=F32)
        pre = top + pltpu.roll(bot, nchunk - 1, 0)
        for g in range(NSA_KV):
            hid = _silu(pre[:, g * CMP_HID:(g + 1) * CMP_HID]).astype(BF16)
            dst[0, g] = jnp.dot(hid, w2_ref[t], preferred_element_type=F32).astype(dst.dtype)


def _pack_compress_weights(pe, w1, w2):
    half = CMP_LEN // 2
    w1r = w1.reshape(2, half, HEAD_DIM, CMP_HID)
    eye = jnp.eye(NSA_KV, dtype=w1.dtype)
    w1p = jnp.einsum("xljc,gk->xlgjkc", w1r, eye).reshape(2, half * LANES, NSA_KV * CMP_HID)
    pep = jnp.tile(pe.reshape(2, half, 1, HEAD_DIM), (1, 1, NSA_KV, 1)).reshape(2, 1, half * LANES)
    w2p = jnp.concatenate([w2, w2], axis=1)
    return pep, w1p.astype(BF16), w2p.astype(BF16)


def _compress(kc_raw, vc_raw, pep, w1p, w2p):
    batch, seq, _ = kc_raw.shape
    nchunk = seq // CMP_STRIDE
    width = CMP_STRIDE * LANES
    view = lambda a: a.reshape(batch, nchunk, width)
    full = lambda a: _resident(a.shape)
    out = jax.ShapeDtypeStruct((batch, NSA_KV, nchunk, LANES), BF16)
    return pl.pallas_call(
        _compress_kernel,
        out_shape=[out, out],
        grid=(batch,),
        in_specs=[pl.BlockSpec((1, nchunk, width), lambda b: (b, 0, 0))] * 2
        + [full(pep), full(w1p), full(w2p)],
        out_specs=[pl.BlockSpec((1, NSA_KV, nchunk, LANES), lambda b: (b, 0, 0, 0))] * 2,
        compiler_params=_params(("arbitrary",)),
        name="nsa_compress",
    )(view(kc_raw), view(vc_raw), pep, w1p, w2p)


def _nsa_kernel(bqr_ref, bqn_ref, blog_ref, kcd_ref, vcd_ref, ksl_ref, vsl_ref, kwn_ref, vwn_ref,
                ovl_ref, y_ref, ksel_s, vsel_s, kwin_s, vwin_s, m_s, l_s, acc_s, *, tile, seq):
    qi = pl.program_id(1)
    q0 = qi * tile
    rows = NSA_REP * tile
    nblk = seq // SEL_LEN
    ncmp = kcd_ref.shape[2]

    @pl.when(qi == 0)
    def _build_kv():
        lane = _lane_iota((seq, LANES))
        lo = lane < HEAD_DIM
        blk_of_row = _row_iota((seq, LANES)) // SEL_LEN
        selmask = jnp.where(blk_of_row == lane - HEAD_DIM, -MASK_BIG, 0.0)
        for src, dst, kind in ((ksl_ref, ksel_s, "ksel"), (vsl_ref, vsel_s, "v"),
                               (kwn_ref, kwin_s, "kwin"), (vwn_ref, vwin_s, "v")):
            a = src[0]
            sw = pltpu.roll(a, HEAD_DIM, 1)
            for g in range(NSA_KV):
                mine, other = (a, sw) if g == 0 else (sw, a)
                if kind == "v":
                    val = jnp.where(lo, mine, other)
                elif kind == "ksel":
                    val = jnp.where(lo, mine, selmask)
                else:
                    val = jnp.where(lo, mine, 0.0)
                dst[g] = val.astype(dst.dtype)

    lane_t = _lane_iota((tile, LANES))
    lo_t = lane_t < HEAD_DIM
    tpos = q0 + _row_iota((tile, LANES))

    def stack_heads(q_ref, g, extra):
        out = []
        for pair in range(2):
            c = g * 256 + pair * LANES
            qp = q_ref[0, :, c:c + LANES].astype(F32)
            out.append(jnp.where(lo_t, qp, extra))
            out.append(jnp.where(lo_t, pltpu.roll(qp, HEAD_DIM, 1), extra))
        return jnp.concatenate(out, axis=0).astype(BF16)

    def flash(qs, k_s, v_s, g, kt_lo, kt_hi, back):
        m_s[...] = jnp.full(m_s.shape, NEG_INF, F32)
        l_s[...] = jnp.zeros(l_s.shape, F32)
        acc_s[...] = jnp.zeros(acc_s.shape, F32)
        qpos = q0 + (_row_iota((rows, tile)) % tile)
        kcol = _lane_iota((rows, tile))

        def step(kt, carry):
            k0 = pl.multiple_of(kt * tile, tile)
            kk = k_s[g, pl.ds(k0, tile), :]
            vv = v_s[g, pl.ds(k0, tile), :]
            s = lax.dot_general(qs, kk, (((1,), (1,)), ((), ())), preferred_element_type=F32)
            dist = qpos - (k0 + kcol)
            s = jnp.where((dist >= 0) & (dist <= back), s, NEG_INF)
            m_old = m_s[...]
            m_new = jnp.maximum(m_old, jnp.max(s, axis=-1, keepdims=True))
            alpha = jnp.exp(m_old - m_new)
            p = jnp.exp(s - m_new)
            l_s[...] = alpha * l_s[...] + jnp.sum(p, axis=-1, keepdims=True)
            acc_s[...] = alpha * acc_s[...] + jnp.dot(p.astype(BF16), vv, preferred_element_type=F32)
            m_s[...] = m_new
            return carry

        lax.fori_loop(kt_lo, kt_hi, step, 0)
        return acc_s[...] * (1.0 / l_s[...])

    gates = _sigmoid(blog_ref[0])
    zero_t = jnp.zeros((tile, LANES), F32)

    for g in range(NSA_KV):
        qn = stack_heads(bqn_ref, g, zero_t)
        s = lax.dot_general(qn, kcd_ref[0, g], (((1,), (1,)), ((), ())), preferred_element_type=F32)
        tq = q0 + (_row_iota((rows, ncmp)) % tile)
        cvis = (_lane_iota((rows, ncmp)) * CMP_STRIDE + (CMP_LEN - 1)) <= tq
        s = jnp.where(cvis, s, NEG_INF)
        m = jnp.max(s, axis=-1, keepdims=True)
        e = jnp.where(cvis, jnp.exp(s - m), 0.0)
        den = jnp.sum(e, axis=-1, keepdims=True)
        p = e * jnp.where(den > 0.0, 1.0 / den, 0.0)
        o_cmp = jnp.dot(p.astype(BF16), vcd_ref[0, g], preferred_element_type=F32)

        psum = p[0:tile] + p[tile:2 * tile] + p[2 * tile:3 * tile] + p[3 * tile:4 * tile]
        p_hi = psum.astype(BF16)
        p_lo = (psum - p_hi.astype(F32)).astype(BF16)
        ovl = ovl_ref[...]
        imp = (jnp.dot(p_hi, ovl, preferred_element_type=F32)
               + jnp.dot(p_lo, ovl, preferred_element_type=F32))
        cur = tpos // SEL_LEN
        forced = (lane_t == 0) | (lane_t == cur) | (lane_t == cur - 1)
        imp = jnp.where(forced, FORCE_SCORE, imp)
        imp = jnp.where(lane_t * SEL_LEN <= tpos, imp, NEG_INF)
        rank = jnp.zeros((tile, LANES), jnp.int32)
        for k in range(nblk):
            vk = imp[:, k:k + 1]
            ahead = (vk > imp) | ((vk == imp) & (lane_t > k))
            rank = rank + ahead.astype(jnp.int32)
        notsel = jnp.where((rank >= SEL_TOPK) & (lane_t < nblk), 1.0, 0.0)
        notsel = pltpu.roll(notsel, HEAD_DIM, 1)

        o_sel = flash(stack_heads(bqr_ref, g, notsel), ksel_s, vsel_s, g, 0, qi + 1, seq)
        win_tiles = -(-(WIN_LEN - 1) // tile)
        o_win = flash(stack_heads(bqr_ref, g, zero_t), kwin_s, vwin_s, g,
                      jnp.maximum(qi - win_tiles, 0), qi + 1, WIN_LEN - 1)

        heads = []
        for r in range(NSA_REP):
            c = (g * NSA_REP + r) * 3
            sl = slice(r * tile, (r + 1) * tile)
            heads.append(gates[:, c:c + 1] * o_cmp[sl] + gates[:, c + 1:c + 2] * o_sel[sl]
                         + gates[:, c + 2:c + 3] * o_win[sl])
        for pair in range(2):
            c = g * 256 + pair * LANES
            y_ref[0, :, c:c + LANES] = jnp.where(lo_t, heads[2 * pair], heads[2 * pair + 1]).astype(y_ref.dtype)


def _overlap_matrix(seq):
    ncmp = seq // CMP_STRIDE
    starts = np.arange(ncmp) * CMP_STRIDE
    bstart = np.arange(LANES) * SEL_LEN
    ovl = (starts[:, None] < bstart[None, :] + SEL_LEN) & (starts[:, None] + CMP_LEN > bstart[None, :])
    ovl &= (np.arange(LANES) < seq // SEL_LEN)[None, :]
    return jnp.asarray(ovl, dtype=BF16)


def _nsa_attention(bqr, bqn, blog, kcd, vcd, ksl, vsl, kwn, vwn):
    batch, seq, _ = bqr.shape
    tile = NSA_TILE
    rows = NSA_REP * tile
    ncmp = seq // CMP_STRIDE
    qspec = lambda w: pl.BlockSpec((1, tile, w), lambda b, i: (b, i, 0))
    kvspec = pl.BlockSpec((1, seq, LANES), lambda b, i: (b, 0, 0))
    cspec = pl.BlockSpec((1, NSA_KV, ncmp, LANES), lambda b, i: (b, 0, 0, 0))
    kv_scratch = pltpu.VMEM((NSA_KV, seq, LANES), BF16)
    return pl.pallas_call(
        functools.partial(_nsa_kernel, tile=tile, seq=seq),
        out_shape=jax.ShapeDtypeStruct((batch, seq, NSA_Q_WIDTH), BF16),
        grid=(batch, seq // tile),
        in_specs=[qspec(NSA_Q_WIDTH), qspec(NSA_Q_WIDTH), qspec(LANES), cspec, cspec,
                  kvspec, kvspec, kvspec, kvspec,
                  pl.BlockSpec((ncmp, LANES), lambda b, i: (0, 0))],
        out_specs=qspec(NSA_Q_WIDTH),
        scratch_shapes=[kv_scratch, kv_scratch, kv_scratch, kv_scratch,
                        pltpu.VMEM((rows, 1), F32), pltpu.VMEM((rows, 1), F32),
                        pltpu.VMEM((rows, LANES), F32)],
        compiler_params=_params(("arbitrary", "arbitrary")),
        name="nsa_attention",
    )(bqr, bqn, blog, kcd, vcd, ksl, vsl, kwn, vwn, _overlap_matrix(seq))


def _merge_kernel(o0_ref, o1_ref, o2_ref, l0_ref, l1_ref, l2_ref, yb_ref, ga_ref, gb_ref, x_ref,
                  gate_ref, wa_ref, wb_ref, wo_ref, out_ref):
    l0, l1, l2 = l0_ref[0], l1_ref[0], l2_ref[0]
    m = jnp.maximum(jnp.maximum(l0, l1), l2)
    e0, e1, e2 = jnp.exp(l0 - m), jnp.exp(l1 - m), jnp.exp(l2 - m)
    inv = 1.0 / (e0 + e1 + e2)
    y_a = (e0 * inv) * o0_ref[0].astype(F32) + (e1 * inv) * o1_ref[0].astype(F32) \
        + (e2 * inv) * o2_ref[0].astype(F32)
    pa = jnp.dot(y_a.astype(BF16), wa_ref[...], preferred_element_type=F32)
    pb = jnp.dot(yb_ref[0], wb_ref[...], preferred_element_type=F32)
    merged = _sigmoid(ga_ref[0].astype(F32)) * pa + _sigmoid(gb_ref[0].astype(F32)) * pb
    out = jnp.dot(merged.astype(BF16), wo_ref[...], preferred_element_type=F32)
    out_ref[0] = x_ref[0] + gate_ref[...] * out


def _merge(o_lse, y_b, ga, gb, x, mod_l, w_a, w_b, w_o):
    batch, seq, d = x.shape
    tm = ROW_TILE
    row = lambda w: pl.BlockSpec((1, tm, w), lambda b, i: (b, i, 0))
    full = lambda a: _resident(a.shape)
    (o0, l0), (o1, l1), (o2, l2) = o_lse
    return pl.pallas_call(
        _merge_kernel,
        out_shape=jax.ShapeDtypeStruct(x.shape, F32),
        grid=(batch, seq // tm),
        in_specs=[row(256)] * 6 + [row(NSA_Q_WIDTH), row(d), row(d), row(d),
                                    pl.BlockSpec((None, None, 1, d), lambda b, i: (b, 2, 0, 0)),
                                    full(w_a), full(w_b), full(w_o)],
        out_specs=row(d),
        compiler_params=_params(("arbitrary", "arbitrary")),
        name="branch_merge",
    )(o0, o1, o2, l0, l1, l2, y_b, ga, gb, x, mod_l, w_a, w_b, w_o)


def _ffn_kernel(x_ref, halo_ref, shift_ref, scale_ref, gate_ref, g_ref, wup_ref, cw_ref, cb_ref,
                wdn_ref, fin_ref, out_ref, *, final):
    i = pl.program_id(1)
    x = x_ref[0]
    tm = x.shape[0]
    xe = jnp.concatenate([halo_ref[0], x], axis=0)
    he = _norm_mod(xe, g_ref[...], scale_ref[...], shift_ref[...]).astype(BF16)
    keep = (_row_iota((tm + HALO, FFN_CHUNK)) + jnp.where(i > 0, HALO, 0)) >= HALO

    def conv(col):
        u = jnp.dot(he, wup_ref[:, col:col + FFN_CHUNK], preferred_element_type=F32)
        u = jnp.where(keep, u, 0.0)
        w = cw_ref[:, col:col + FFN_CHUNK]
        y = (w[0:1] * pltpu.roll(u, 2, 0)[HALO:] + w[1:2] * pltpu.roll(u, 1, 0)[HALO:]
             + w[2:3] * u[HALO:])
        return y + cb_ref[:, col:col + FFN_CHUNK]

    acc = jnp.zeros((tm, x.shape[1]), F32)
    for c in range(0, D_FF, FFN_CHUNK):
        act = (_silu(conv(c)) * conv(D_FF + c)).astype(BF16)
        acc = acc + jnp.dot(act, wdn_ref[c:c + FFN_CHUNK, :], preferred_element_type=F32)
    y = x + gate_ref[...] * acc
    if final:
        ms = jnp.mean(y * y, axis=-1, keepdims=True)
        y = y * lax.rsqrt(ms + RMS_EPS) * fin_ref[...]
    out_ref[0] = y


def _ffn(x, mod_l, norm_g, w_up, conv_w, conv_b, w_down, final_g, final):
    batch, seq, d = x.shape
    tm = ROW_TILE
    row = pl.BlockSpec((1, tm, d), lambda b, i: (b, i, 0))
    halo = pl.BlockSpec((1, HALO, d), lambda b, i: (b, jnp.maximum(i * (tm // HALO) - 1, 0), 0))
    vec = lambda k: pl.BlockSpec((None, None, 1, d), lambda b, i, k=k: (b, k, 0, 0))
    full = lambda a: _resident(a.shape)
    norm_g = norm_g.reshape(1, d)
    conv_b = conv_b.reshape(1, -1)
    final_g = final_g.reshape(1, d)
    return pl.pallas_call(
        functools.partial(_ffn_kernel, final=final),
        out_shape=jax.ShapeDtypeStruct(x.shape, F32),
        grid=(batch, seq // tm),
        in_specs=[row, halo, vec(3), vec(4), vec(5), full(norm_g), full(w_up), full(conv_w),
                  full(conv_b), full(w_down), full(final_g)],
        out_specs=row,
        compiler_params=_params(("arbitrary", "arbitrary")),
        name="conv_ffn",
    )(x, x, mod_l, mod_l, mod_l, norm_g, w_up, conv_w, conv_b, w_down, final_g)


def _rope_tables(seq):
    inv = 1.0 / (ROPE_THETA ** (jnp.arange(0, HEAD_DIM, 2, dtype=F32) / HEAD_DIM))
    ang = jnp.arange(seq, dtype=F32)[:, None] * inv[None, :]
    cos, sin = jnp.cos(ang), jnp.sin(ang)
    cos_t = jnp.tile(cos, (1, LANES // (HEAD_DIM // 2)))
    sin_t = jnp.tile(jnp.concatenate([-sin, sin], axis=1), (1, LANES // HEAD_DIM))
    return cos_t, sin_t


def kernel(x, c, norm1_g, norm2_g, final_g, w_mod, b_mod, w_in, cmp_pe_k, cmp_pe_v, cmp_w1_k, cmp_w2_k, cmp_w1_v, cmp_w2_v, w_br_a, w_br_b, w_out, w_up, conv_w, conv_b, w_down):
    batch, seq, d = x.shape
    depth = w_in.shape[0]
    cos_t, sin_t = _rope_tables(seq)
    mod = _modulation(c, w_mod, b_mod).reshape(depth, batch, N_MOD, 1, d)
    for layer in range(depth):
        mod_l = mod[layer]
        (qkv0, qkv1, qkv2, bqr, bqn, kc_raw, vc_raw, ksl, vsl, kwn, vwn, blog, ga, gb) = _input_projection(
            x, mod_l, norm1_g[layer], cos_t, sin_t, _pack_w_in(w_in[layer]))
        o_lse = [_banded_attention(qkv, dil) for qkv, (_, dil) in zip((qkv0, qkv1, qkv2), DIL_GROUPS)]

        pk = _pack_compress_weights(cmp_pe_k[layer], cmp_w1_k[layer], cmp_w2_k[layer])
        pv = _pack_compress_weights(cmp_pe_v[layer], cmp_w1_v[layer], cmp_w2_v[layer])
        pep, w1p, w2p = (jnp.stack([a, b]) for a, b in zip(pk, pv))
        kcd, vcd = _compress(kc_raw, vc_raw, pep, w1p, w2p)
        y_b = _nsa_attention(bqr, bqn, blog, kcd, vcd, ksl, vsl, kwn, vwn)

        x = _merge(o_lse, y_b, ga, gb, x, mod_l, w_br_a[layer].astype(BF16),
                   w_br_b[layer].astype(BF16), w_out[layer].astype(BF16))
        x = _ffn(x, mod_l, norm2_g[layer], w_up[layer].astype(BF16), conv_w[layer], conv_b[layer],
                 w_down[layer].astype(BF16), final_g, final=(layer == depth - 1))
    return x
```

```python
import functools

import jax
import jax.numpy as jnp
import numpy as np
from jax import lax
from jax.experimental import pallas as pl
from jax.experimental.pallas import tpu as pltpu

F32 = jnp.float32
BF16 = jnp.bfloat16

HEAD_DIM = 64
ROPE_THETA = 10000.0
RMS_EPS = 1e-6
NEG_INF = -1e30
FORCE_SCORE = 1e9
MASK_BIG = float(2.0 ** 100)

DIL_GROUPS = ((128, 1), (512, 4), (2048, 16))
A_HEADS = 4
A_GROUP_WIDTH = A_HEADS * HEAD_DIM
BAND_BACK = 128
NSA_HEADS = 8
NSA_KV = 2
NSA_REP = NSA_HEADS // NSA_KV
NSA_Q_WIDTH = NSA_HEADS * HEAD_DIM
CMP_LEN = 32
CMP_STRIDE = 16
CMP_HID = 256
SEL_LEN = 64
SEL_TOPK = 16
WIN_LEN = 512
D_FF = 2816
CONV_W = 3
N_MOD = 6

LANES = 128
VMEM_LIMIT = 56 * 1024 * 1024

ROW_TILE = 512
NSA_TILE = 256
FFN_CHUNK = 256
HALO = 8
V_ROWS = 80


def _sigmoid(x):
    return 1.0 / (1.0 + jnp.exp(-x))


def _silu(x):
    return x * _sigmoid(x)


def _params(semantics):
    return pltpu.CompilerParams(dimension_semantics=semantics, vmem_limit_bytes=VMEM_LIMIT)


def _resident(shape):
    nd = len(shape)
    return pl.BlockSpec(tuple(shape), lambda *_: (0,) * nd, pipeline_mode=pl.Buffered(1))


def _lane_iota(shape):
    return lax.broadcasted_iota(jnp.int32, shape, len(shape) - 1)


def _row_iota(shape):
    return lax.broadcasted_iota(jnp.int32, shape, len(shape) - 2)


def _mod_kernel(c_ref, w_ref, b_ref, o_ref):
    sc = _silu(c_ref[...])
    o_ref[0] = jnp.dot(sc, w_ref[0], preferred_element_type=F32) + b_ref[0]


def _modulation(c, w_mod, b_mod):
    depth, d, n = w_mod.shape
    batch = c.shape[0]
    tn = 1024
    return pl.pallas_call(
        _mod_kernel,
        out_shape=jax.ShapeDtypeStruct((depth, batch, n), F32),
        grid=(depth, n // tn),
        in_specs=[
            pl.BlockSpec((batch, d), lambda l, j: (0, 0)),
            pl.BlockSpec((1, d, tn), lambda l, j: (l, 0, j)),
            pl.BlockSpec((1, 1, tn), lambda l, j: (l, 0, j)),
        ],
        out_specs=pl.BlockSpec((1, batch, tn), lambda l, j: (l, 0, j)),
        compiler_params=_params(("arbitrary", "arbitrary")),
        name="modulation",
    )(c, w_mod, b_mod.reshape(depth, 1, n))


def _norm_mod(x, g, scale, shift):
    ms = jnp.mean(x * x, axis=-1, keepdims=True)
    y = x * lax.rsqrt(ms + RMS_EPS) * g
    return y * (1.0 + scale) + shift


def _rope_chunk(xc, cos, sin_signed, first_half):
    partner = jnp.where(first_half, pltpu.roll(xc, LANES - HEAD_DIM // 2, 1),
                        pltpu.roll(xc, HEAD_DIM // 2, 1))
    return xc * cos + partner * sin_signed


_QKV_A = 3 * A_GROUP_WIDTH
_OFF_BQ = 3 * _QKV_A
_OFF_KV = _OFF_BQ + NSA_Q_WIDTH
_OFF_BLOG = _OFF_KV + 6 * LANES
_OFF_GA = _OFF_BLOG + LANES
N_IN_PACKED = _OFF_GA + 2 * 1024


def _inproj_kernel(x_ref, shift_ref, scale_ref, g_ref, cos_ref, sin_ref, w_ref,
                   qkv0_ref, qkv1_ref, qkv2_ref, bqr_ref, bqn_ref, kc_ref, vc_ref,
                   ksl_ref, vsl_ref, kwn_ref, vwn_ref, blog_ref, ga_ref, gb_ref):
    h = _norm_mod(x_ref[0], g_ref[...], scale_ref[...], shift_ref[...]).astype(BF16)
    cos = cos_ref[...]
    sin = sin_ref[...]
    rows = cos.shape[0]
    first_half = (_lane_iota((rows, LANES)) % HEAD_DIM) < (HEAD_DIM // 2)
    scale = HEAD_DIM ** -0.5

    def proj(off, width):
        return jnp.dot(h, w_ref[:, off:off + width], preferred_element_type=F32)

    def rope(y, mult):
        parts = [_rope_chunk(y[:, c:c + LANES], cos, sin, first_half) * mult
                 for c in range(0, y.shape[1], LANES)]
        return parts[0] if len(parts) == 1 else jnp.concatenate(parts, axis=1)

    for g, out_ref in enumerate((qkv0_ref, qkv1_ref, qkv2_ref)):
        base = g * _QKV_A
        out_ref[0, :, 0:256] = rope(proj(base, 256), scale).astype(out_ref.dtype)
        out_ref[0, :, 256:512] = rope(proj(base + 256, 256), 1.0).astype(out_ref.dtype)
        out_ref[0, :, 512:768] = proj(base + 512, 256).astype(out_ref.dtype)

    bq = proj(_OFF_BQ, NSA_Q_WIDTH)
    bqn_ref[0] = (bq * scale).astype(bqn_ref.dtype)
    bqr_ref[0] = rope(bq, scale).astype(bqr_ref.dtype)

    kv = proj(_OFF_KV, 6 * LANES)
    kc_ref[0] = kv[:, 0:128].astype(kc_ref.dtype)
    vc_ref[0] = kv[:, 128:256].astype(vc_ref.dtype)
    ksl_ref[0] = rope(kv[:, 256:384], 1.0).astype(ksl_ref.dtype)
    vsl_ref[0] = kv[:, 384:512].astype(vsl_ref.dtype)
    kwn_ref[0] = rope(kv[:, 512:640], 1.0).astype(kwn_ref.dtype)
    vwn_ref[0] = kv[:, 640:768].astype(vwn_ref.dtype)

    blog_ref[0] = proj(_OFF_BLOG, LANES)
    ga_ref[0] = proj(_OFF_GA, 1024).astype(ga_ref.dtype)
    gb_ref[0] = proj(_OFF_GA + 1024, 1024).astype(gb_ref.dtype)


def _pack_w_in(w):
    aq, ak, av = w[:, 0:768], w[:, 768:1536], w[:, 1536:2304]
    cols = []
    for g in range(3):
        sl = slice(g * 256, (g + 1) * 256)
        cols += [aq[:, sl], ak[:, sl], av[:, sl]]
    cols.append(w[:, 2304:3584])
    cols.append(jnp.pad(w[:, 3584:3608], ((0, 0), (0, LANES - 24))))
    cols.append(w[:, 3608:5656])
    return jnp.concatenate(cols, axis=1).astype(BF16)


def _input_projection(x, mod_l, norm_g, cos_t, sin_t, w_packed):
    batch, seq, d = x.shape
    tm = ROW_TILE
    row = lambda w: pl.BlockSpec((1, tm, w), lambda b, i: (b, i, 0))
    vec = lambda k: pl.BlockSpec((None, None, 1, d), lambda b, i, k=k: (b, k, 0, 0))
    shp = lambda w, dt: jax.ShapeDtypeStruct((batch, seq, w), dt)
    return pl.pallas_call(
        _inproj_kernel,
        out_shape=[shp(768, BF16)] * 3 + [shp(512, BF16)] * 2 + [shp(128, BF16)] * 2
        + [shp(128, F32)] * 4 + [shp(128, F32)] + [shp(1024, BF16)] * 2,
        grid=(batch, seq // tm),
        in_specs=[
            row(d), vec(0), vec(1),
            pl.BlockSpec((1, d), lambda b, i: (0, 0)),
            pl.BlockSpec((tm, LANES), lambda b, i: (i, 0)),
            pl.BlockSpec((tm, LANES), lambda b, i: (i, 0)),
            _resident((d, N_IN_PACKED)),
        ],
        out_specs=[row(768)] * 3 + [row(512)] * 2 + [row(128)] * 7 + [row(1024)] * 2,
        compiler_params=_params(("arbitrary", "arbitrary")),
        name="input_projection",
    )(x, mod_l, mod_l, norm_g.reshape(1, d), cos_t, sin_t, w_packed)


def _banded_kernel(qkv_ref, o_ref, lse_ref, *, length, nres):
    blk = 128
    nkeys = 2 * blk if length > blk else blk
    lane = _lane_iota((blk, LANES))
    lo = lane < HEAD_DIM
    qrow = _row_iota((2 * blk, nkeys)) % blk
    kcol = _lane_iota((2 * blk, nkeys))

    def unit(i, res, pair):
        base = res * _QKV_A + pair * LANES
        r0 = pl.multiple_of(i * blk, blk)
        qp = qkv_ref[0, pl.ds(r0, blk), base:base + LANES]
        if length > blk:
            ks = pl.multiple_of(jnp.maximum(r0 - blk, 0), blk)
        else:
            ks = 0
        k2 = qkv_ref[0, pl.ds(ks, nkeys), base + 256:base + 256 + LANES]
        v2 = qkv_ref[0, pl.ds(ks, nkeys), base + 512:base + 512 + LANES]
        zero = jnp.zeros_like(qp)
        qs = jnp.concatenate([jnp.where(lo, qp, zero), jnp.where(lo, zero, qp)], axis=0)
        s = lax.dot_general(qs, k2, (((1,), (1,)), ((), ())), preferred_element_type=F32)
        dist = (r0 + qrow) - (ks + kcol)
        s = jnp.where((dist >= 0) & (dist <= BAND_BACK), s, NEG_INF)
        m = jnp.max(s, axis=-1, keepdims=True)
        e = jnp.exp(s - m)
        den = jnp.sum(e, axis=-1, keepdims=True)
        o = jnp.dot(e.astype(BF16), v2, preferred_element_type=F32) * (1.0 / den)
        lse = m + jnp.log(den)
        ob = res * A_GROUP_WIDTH + pair * LANES
        o_ref[0, pl.ds(r0, blk), ob:ob + LANES] = jnp.where(lo, o[:blk], o[blk:]).astype(o_ref.dtype)
        lse_ref[0, pl.ds(r0, blk), ob:ob + LANES] = jnp.where(
            lo, jnp.broadcast_to(lse[:blk], (blk, LANES)), jnp.broadcast_to(lse[blk:], (blk, LANES)))

    def body(i, carry):
        for res in range(nres):
            for pair in range(2):
                unit(i, res, pair)
        return carry

    lax.fori_loop(0, length // blk, body, 0)


def _banded_attention(qkv, dil):
    batch, seq, _ = qkv.shape
    length = seq // dil
    nres = min(dil, 4)
    view = qkv.reshape(batch, length, dil * _QKV_A)
    o, lse = pl.pallas_call(
        functools.partial(_banded_kernel, length=length, nres=nres),
        out_shape=[jax.ShapeDtypeStruct((batch, length, dil * A_GROUP_WIDTH), BF16),
                   jax.ShapeDtypeStruct((batch, length, dil * A_GROUP_WIDTH), F32)],
        grid=(batch, dil // nres),
        in_specs=[pl.BlockSpec((1, length, nres * _QKV_A), lambda b, r: (b, 0, r))],
        out_specs=[pl.BlockSpec((1, length, nres * A_GROUP_WIDTH), lambda b, r: (b, 0, r))] * 2,
        compiler_params=_params(("arbitrary", "arbitrary")),
        name=f"banded_attention_d{dil}",
    )(view)
    return o.reshape(batch, seq, A_GROUP_WIDTH), lse.reshape(batch, seq, A_GROUP_WIDTH)


def _compress_kernel(k_ref, v_ref, pe_ref, w1_ref, w2_ref, kc_ref, vc_ref):
    nchunk = k_ref.shape[1]
    for t, (src, dst) in enumerate(((k_ref, kc_ref), (v_ref, vc_ref))):
        c = src[0].astype(F32)
        top = jnp.dot((c + pe_ref[t, 0]).astype(BF16), w1_ref[t, 0], preferred_element_type=F32)
        bot = jnp.dot((c + pe_ref[t, 1]).astype(BF16), w1_ref[t, 1], preferred_element_type=F32)
        pre = top + pltpu.roll(bot, nchunk - 1, 0)
        for g in range(NSA_KV):
            hid = _silu(pre[:, g * CMP_HID:(g + 1) * CMP_HID]).astype(BF16)
            res = jnp.dot(hid, w2_ref[t], preferred_element_type=F32)
            dst[0, g] = (res if t == 0 else res.T).astype(dst.dtype)


def _pack_compress_weights(pe, w1, w2):
    half = CMP_LEN // 2
    w1r = w1.reshape(2, half, HEAD_DIM, CMP_HID)
    eye = jnp.eye(NSA_KV, dtype=w1.dtype)
    w1p = jnp.einsum("xljc,gk->xlgjkc", w1r, eye).reshape(2, half * LANES, NSA_KV * CMP_HID)
    pep = jnp.tile(pe.reshape(2, half, 1, HEAD_DIM), (1, 1, NSA_KV, 1)).reshape(2, 1, half * LANES)
    w2p = jnp.concatenate([w2, w2], axis=1)
    return pep, w1p.astype(BF16), w2p.astype(BF16)


def _compress(kc_raw, vc_raw, pep, w1p, w2p):
    batch, seq, _ = kc_raw.shape
    nchunk = seq // CMP_STRIDE
    width = CMP_STRIDE * LANES
    view = lambda a: a.reshape(batch, nchunk, width)
    full = lambda a: _resident(a.shape)
    out = jax.ShapeDtypeStruct((batch, NSA_KV, nchunk, LANES), BF16)
    return pl.pallas_call(
        _compress_kernel,
        out_shape=[out, out],
        grid=(batch,),
        in_specs=[pl.BlockSpec((1, nchunk, width), lambda b: (b, 0, 0))] * 2
        + [full(pep), full(w1p), full(w2p)],
        out_specs=[pl.BlockSpec((1, NSA_KV, nchunk, LANES), lambda b: (b, 0, 0, 0))] * 2,
        compiler_params=_params(("arbitrary",)),
        name="nsa_compress",
    )(view(kc_raw), view(vc_raw), pep, w1p, w2p)


def _nsa_kernel(bqr_ref, bqn_ref, blog_ref, kcd_ref, vct_ref, ksl_ref, vsl_ref, kwn_ref, vwn_ref,
                ovl_ref, y_ref, ksel_s, vsel_s, kwin_s, vwin_s, m_s, acc_s, *, tile, seq):
    qi = pl.program_id(1)
    q0 = qi * tile
    cols = NSA_REP * tile
    nblk = seq // SEL_LEN
    ncmp = kcd_ref.shape[2]
    ntile = seq // tile
    nt = (((1,), (1,)), ((), ()))

    @pl.when(qi == 0)
    def _build_kv():
        lane = _lane_iota((seq, LANES))
        lo = lane < HEAD_DIM
        blk_of_row = _row_iota((seq, LANES)) // SEL_LEN
        selmask = jnp.where(blk_of_row == lane - HEAD_DIM, -MASK_BIG, 0.0)
        for src, dst, fill in ((ksl_ref, ksel_s, selmask), (kwn_ref, kwin_s, 0.0)):
            a = src[0]
            dst[0] = jnp.where(lo, a, fill).astype(dst.dtype)
            dst[1] = jnp.where(lo, pltpu.roll(a, HEAD_DIM, 1), fill).astype(dst.dtype)
        ones_rows = jnp.where(_row_iota((V_ROWS - HEAD_DIM, seq)) == 0, 1.0, 0.0)
        for src, dst in ((vsl_ref, vsel_s), (vwn_ref, vwin_s)):
            vt = src[0].T
            for g in range(NSA_KV):
                full = jnp.concatenate([vt[g * HEAD_DIM:(g + 1) * HEAD_DIM], ones_rows], axis=0).astype(dst.dtype)
                for kt in range(ntile):
                    dst[g, kt] = full[:, kt * tile:(kt + 1) * tile]

    lo_t = _lane_iota((tile, LANES)) < HEAD_DIM

    def stack_heads(q_ref, g, extra):
        out = []
        for pair in range(2):
            c = g * 256 + pair * LANES
            qp = q_ref[0, :, c:c + LANES].astype(F32)
            out.append(jnp.where(lo_t, qp, extra))
            out.append(jnp.where(lo_t, pltpu.roll(qp, HEAD_DIM, 1), extra))
        return jnp.concatenate(out, axis=0).astype(BF16)

    qpos = q0 + (_lane_iota((tile, cols)) % tile)
    krow = _row_iota((tile, cols))

    def flash_step(qs, k_s, v_s, g, kt, mask):
        k0 = pl.multiple_of(kt * tile, tile)
        s = lax.dot_general(k_s[g, pl.ds(k0, tile), :], qs, nt, preferred_element_type=F32)
        if mask == "causal":
            s = jnp.where(qpos >= k0 + krow, s, NEG_INF)
        elif mask == "band":
            s = jnp.where(qpos - (k0 + krow) <= WIN_LEN - 1, s, NEG_INF)
        m_old = m_s[...]
        m_new = jnp.maximum(m_old, jnp.max(s, axis=0, keepdims=True))
        alpha = jnp.exp(m_old - m_new)
        p = jnp.exp(s - m_new).astype(BF16)
        acc_s[...] = alpha * acc_s[...] + jnp.dot(v_s[g, kt], p, preferred_element_type=F32)
        m_s[...] = m_new

    def flash_init():
        m_s[...] = jnp.full(m_s.shape, NEG_INF, F32)
        acc_s[...] = jnp.zeros(acc_s.shape, F32)

    def flash_result():
        acc = acc_s[...]
        return acc[0:HEAD_DIM] * (1.0 / acc[HEAD_DIM:HEAD_DIM + 1])

    gates = _sigmoid(blog_ref[0].T)
    zero_t = jnp.zeros((tile, LANES), F32)
    blk = _row_iota((nblk, tile))
    tpos = q0 + _lane_iota((nblk, tile))
    heads = []

    for g in range(NSA_KV):
        qn = stack_heads(bqn_ref, g, zero_t)
        s = lax.dot_general(kcd_ref[0, g], qn, nt, preferred_element_type=F32)
        tq = q0 + (_lane_iota((ncmp, cols)) % tile)
        cvis = (_row_iota((ncmp, cols)) * CMP_STRIDE + (CMP_LEN - 1)) <= tq
        s = jnp.where(cvis, s, NEG_INF)
        m = jnp.max(s, axis=0, keepdims=True)
        e = jnp.where(cvis, jnp.exp(s - m), 0.0)
        den = jnp.sum(e, axis=0, keepdims=True)
        p = e * jnp.where(den > 0.0, 1.0 / den, 0.0)
        o_cmp = jnp.dot(vct_ref[0, g, 0:HEAD_DIM, :], p.astype(BF16), preferred_element_type=F32)

        psum = p[:, 0:tile] + p[:, tile:2 * tile] + p[:, 2 * tile:3 * tile] + p[:, 3 * tile:4 * tile]
        p_hi = psum.astype(BF16)
        p_lo = (psum - p_hi.astype(F32)).astype(BF16)
        ovl = ovl_ref[...]
        imp = (jnp.dot(ovl, p_hi, preferred_element_type=F32)
               + jnp.dot(ovl, p_lo, preferred_element_type=F32))
        cur = tpos // SEL_LEN
        forced = (blk == 0) | (blk == cur) | (blk == cur - 1)
        imp = jnp.where(forced, FORCE_SCORE, imp)
        imp = jnp.where(blk * SEL_LEN <= tpos, imp, NEG_INF)
        rank = jnp.zeros((nblk, tile), jnp.int32)
        for k in range(nblk):
            vk = imp[k:k + 1, :]
            ahead = (vk > imp) | ((vk == imp) & (blk > k))
            rank = rank + ahead.astype(jnp.int32)
        notsel = jnp.where(rank >= SEL_TOPK, 1.0, 0.0)
        extra = jnp.concatenate([jnp.zeros((HEAD_DIM, tile), F32), notsel,
                                 jnp.zeros((LANES - HEAD_DIM - nblk, tile), F32)], axis=0).T

        qs = stack_heads(bqr_ref, g, extra)
        flash_init()
        lax.fori_loop(0, qi, lambda kt, c: (flash_step(qs, ksel_s, vsel_s, g, kt, None), c)[1], 0)
        flash_step(qs, ksel_s, vsel_s, g, qi, "causal")
        o_sel = flash_result()

        qw = stack_heads(bqr_ref, g, zero_t)
        flash_init()

        @pl.when(qi >= 2)
        def _():
            flash_step(qw, kwin_s, vwin_s, g, qi - 2, "band")

        @pl.when(qi >= 1)
        def _():
            flash_step(qw, kwin_s, vwin_s, g, qi - 1, None)

        flash_step(qw, kwin_s, vwin_s, g, qi, "causal")
        o_win = flash_result()

        for r in range(NSA_REP):
            c = (g * NSA_REP + r) * 3
            sl = slice(r * tile, (r + 1) * tile)
            heads.append(gates[c:c + 1] * o_cmp[:, sl] + gates[c + 1:c + 2] * o_sel[:, sl]
                         + gates[c + 2:c + 3] * o_win[:, sl])

    y_ref[0] = jnp.concatenate(heads, axis=0).T.astype(y_ref.dtype)


def _overlap_matrix(seq):
    starts = np.arange(seq // CMP_STRIDE) * CMP_STRIDE
    bstart = np.arange(seq // SEL_LEN) * SEL_LEN
    ovl = (starts[None, :] < bstart[:, None] + SEL_LEN) & (starts[None, :] + CMP_LEN > bstart[:, None])
    return jnp.asarray(ovl, dtype=BF16)


def _nsa_attention(bqr, bqn, blog, kcd, vct, ksl, vsl, kwn, vwn):
    batch, seq, _ = bqr.shape
    tile = NSA_TILE
    assert 2 * tile == WIN_LEN
    cols = NSA_REP * tile
    ncmp = seq // CMP_STRIDE
    nblk = seq // SEL_LEN
    qspec = lambda w: pl.BlockSpec((1, tile, w), lambda b, i: (b, i, 0))
    kvspec = pl.BlockSpec((1, seq, LANES), lambda b, i: (b, 0, 0))
    cspec = pl.BlockSpec((1, NSA_KV, ncmp, LANES), lambda b, i: (b, 0, 0, 0))
    k_scratch = pltpu.VMEM((NSA_KV, seq, LANES), BF16)
    v_scratch = pltpu.VMEM((NSA_KV, seq // tile, V_ROWS, tile), BF16)
    return pl.pallas_call(
        functools.partial(_nsa_kernel, tile=tile, seq=seq),
        out_shape=jax.ShapeDtypeStruct((batch, seq, NSA_Q_WIDTH), BF16),
        grid=(batch, seq // tile),
        in_specs=[qspec(NSA_Q_WIDTH), qspec(NSA_Q_WIDTH), qspec(LANES), cspec, cspec,
                  kvspec, kvspec, kvspec, kvspec,
                  pl.BlockSpec((nblk, ncmp), lambda b, i: (0, 0))],
        out_specs=qspec(NSA_Q_WIDTH),
        scratch_shapes=[k_scratch, v_scratch, k_scratch, v_scratch,
                        pltpu.VMEM((1, cols), F32), pltpu.VMEM((V_ROWS, cols), F32)],
        compiler_params=_params(("arbitrary", "arbitrary")),
        name="nsa_attention",
    )(bqr, bqn, blog, kcd, vct, ksl, vsl, kwn, vwn, _overlap_matrix(seq))


def _merge_kernel(o0_ref, o1_ref, o2_ref, l0_ref, l1_ref, l2_ref, yb_ref, ga_ref, gb_ref, x_ref,
                  gate_ref, wa_ref, wb_ref, wo_ref, out_ref):
    l0, l1, l2 = l0_ref[0], l1_ref[0], l2_ref[0]
    m = jnp.maximum(jnp.maximum(l0, l1), l2)
    e0, e1, e2 = jnp.exp(l0 - m), jnp.exp(l1 - m), jnp.exp(l2 - m)
    inv = 1.0 / (e0 + e1 + e2)
    y_a = (e0 * inv) * o0_ref[0].astype(F32) + (e1 * inv) * o1_ref[0].astype(F32) \
        + (e2 * inv) * o2_ref[0].astype(F32)
    pa = jnp.dot(y_a.astype(BF16), wa_ref[...], preferred_element_type=F32)
    pb = jnp.dot(yb_ref[0], wb_ref[...], preferred_element_type=F32)
    merged = _sigmoid(ga_ref[0].astype(F32)) * pa + _sigmoid(gb_ref[0].astype(F32)) * pb
    out = jnp.dot(merged.astype(BF16), wo_ref[...], preferred_element_type=F32)
    out_ref[0] = x_ref[0] + gate_ref[...] * out


def _merge(o_lse, y_b, ga, gb, x, mod_l, w_a, w_b, w_o):
    batch, seq, d = x.shape
    tm = ROW_TILE
    row = lambda w: pl.BlockSpec((1, tm, w), lambda b, i: (b, i, 0))
    full = lambda a: _resident(a.shape)
    (o0, l0), (o1, l1), (o2, l2) = o_lse
    return pl.pallas_call(
        _merge_kernel,
        out_shape=jax.ShapeDtypeStruct(x.shape, F32),
        grid=(batch, seq // tm),
        in_specs=[row(256)] * 6 + [row(NSA_Q_WIDTH), row(d), row(d), row(d),
                                    pl.BlockSpec((None, None, 1, d), lambda b, i: (b, 2, 0, 0)),
                                    full(w_a), full(w_b), full(w_o)],
        out_specs=row(d),
        compiler_params=_params(("arbitrary", "arbitrary")),
        name="branch_merge",
    )(o0, o1, o2, l0, l1, l2, y_b, ga, gb, x, mod_l, w_a, w_b, w_o)


def _ffn_kernel(x_ref, halo_ref, shift_ref, scale_ref, gate_ref, g_ref, wup_ref, cw_ref, cb_ref,
                wdn_ref, fin_ref, out_ref, *, final):
    i = pl.program_id(1)
    x = x_ref[0]
    tm = x.shape[0]
    xe = jnp.concatenate([halo_ref[0], x], axis=0)
    he = _norm_mod(xe, g_ref[...], scale_ref[...], shift_ref[...]).astype(BF16)
    keep = (_row_iota((tm + HALO, FFN_CHUNK)) + jnp.where(i > 0, HALO, 0)) >= HALO

    def conv(col):
        u = jnp.dot(he, wup_ref[:, col:col + FFN_CHUNK], preferred_element_type=F32)
        u = jnp.where(keep, u, 0.0)
        w = cw_ref[:, col:col + FFN_CHUNK]
        y = (w[0:1] * pltpu.roll(u, 2, 0)[HALO:] + w[1:2] * pltpu.roll(u, 1, 0)[HALO:]
             + w[2:3] * u[HALO:])
        return y + cb_ref[:, col:col + FFN_CHUNK]

    acc = jnp.zeros((tm, x.shape[1]), F32)
    for c in range(0, D_FF, FFN_CHUNK):
        act = (_silu(conv(c)) * conv(D_FF + c)).astype(BF16)
        acc = acc + jnp.dot(act, wdn_ref[c:c + FFN_CHUNK, :], preferred_element_type=F32)
    y = x + gate_ref[...] * acc
    if final:
        ms = jnp.mean(y * y, axis=-1, keepdims=True)
        y = y * lax.rsqrt(ms + RMS_EPS) * fin_ref[...]
    out_ref[0] = y


def _ffn(x, mod_l, norm_g, w_up, conv_w, conv_b, w_down, final_g, final):
    batch, seq, d = x.shape
    tm = ROW_TILE
    row = pl.BlockSpec((1, tm, d), lambda b, i: (b, i, 0))
    halo = pl.BlockSpec((1, HALO, d), lambda b, i: (b, jnp.maximum(i * (tm // HALO) - 1, 0), 0))
    vec = lambda k: pl.BlockSpec((None, None, 1, d), lambda b, i, k=k: (b, k, 0, 0))
    full = lambda a: _resident(a.shape)
    norm_g = norm_g.reshape(1, d)
    conv_b = conv_b.reshape(1, -1)
    final_g = final_g.reshape(1, d)
    return pl.pallas_call(
        functools.partial(_ffn_kernel, final=final),
        out_shape=jax.ShapeDtypeStruct(x.shape, F32),
        grid=(batch, seq // tm),
        in_specs=[row, halo, vec(3), vec(4), vec(5), full(norm_g), full(w_up), full(conv_w),
                  full(conv_b), full(w_down), full(final_g)],
        out_specs=row,
        compiler_params=_params(("arbitrary", "arbitrary")),
        name="conv_ffn",
    )(x, x, mod_l, mod_l, mod_l, norm_g, w_up, conv_w, conv_b, w_down, final_g)


def _rope_tables(seq):
    inv = 1.0 / (ROPE_THETA ** (jnp.arange(0, HEAD_DIM, 2, dtype=F32) / HEAD_DIM))
    ang = jnp.arange(seq, dtype=F32)[:, None] * inv[None, :]
    cos, sin = jnp.cos(ang), jnp.sin(ang)
    cos_t = jnp.tile(cos, (1, LANES // (HEAD_DIM // 2)))
    sin_t = jnp.tile(jnp.concatenate([-sin, sin], axis=1), (1, LANES // HEAD_DIM))
    return cos_t, sin_t


def kernel(x, c, norm1_g, norm2_g, final_g, w_mod, b_mod, w_in, cmp_pe_k, cmp_pe_v, cmp_w1_k, cmp_w2_k, cmp_w1_v, cmp_w2_v, w_br_a, w_br_b, w_out, w_up, conv_w, conv_b, w_down):
    batch, seq, d = x.shape
    depth = w_in.shape[0]
    cos_t, sin_t = _rope_tables(seq)
    mod = _modulation(c, w_mod, b_mod).reshape(depth, batch, N_MOD, 1, d)
    for layer in range(depth):
        mod_l = mod[layer]
        (qkv0, qkv1, qkv2, bqr, bqn, kc_raw, vc_raw, ksl, vsl, kwn, vwn, blog, ga, gb) = _input_projection(
            x, mod_l, norm1_g[layer], cos_t, sin_t, _pack_w_in(w_in[layer]))
        o_lse = [_banded_attention(qkv, dil) for qkv, (_, dil) in zip((qkv0, qkv1, qkv2), DIL_GROUPS)]

        pk = _pack_compress_weights(cmp_pe_k[layer], cmp_w1_k[layer], cmp_w2_k[layer])
        pv = _pack_compress_weights(cmp_pe_v[layer], cmp_w1_v[layer], cmp_w2_v[layer])
        pep, w1p, w2p = (jnp.stack([a, b]) for a, b in zip(pk, pv))
        kcd, vcd = _compress(kc_raw, vc_raw, pep, w1p, w2p)
        y_b = _nsa_attention(bqr, bqn, blog, kcd, vcd, ksl, vsl, kwn, vwn)

        x = _merge(o_lse, y_b, ga, gb, x, mod_l, w_br_a[layer].astype(BF16),
                   w_br_b[layer].astype(BF16), w_out[layer].astype(BF16))
        x = _ffn(x, mod_l, norm2_g[layer], w_up[layer].astype(BF16), conv_w[layer], conv_b[layer],
                 w_down[layer].astype(BF16), final_g, final=(layer == depth - 1))
    return x
```

```python
import functools

import jax
import jax.numpy as jnp
import numpy as np
from jax import lax
from jax.experimental import pallas as pl
from jax.experimental.pallas import tpu as pltpu

F32 = jnp.float32
BF16 = jnp.bfloat16

HEAD_DIM = 64
ROPE_THETA = 10000.0
RMS_EPS = 1e-6
NEG_INF = -1e30
FORCE_SCORE = 1e9
MASK_BIG = float(2.0 ** 100)

DIL_GROUPS = ((128, 1), (512, 4), (2048, 16))
A_HEADS = 4
A_GROUP_WIDTH = A_HEADS * HEAD_DIM
BAND_BACK = 128
NSA_HEADS = 8
NSA_KV = 2
NSA_REP = NSA_HEADS // NSA_KV
NSA_Q_WIDTH = NSA_HEADS * HEAD_DIM
CMP_LEN = 32
CMP_STRIDE = 16
CMP_HID = 256
SEL_LEN = 64
SEL_TOPK = 16
WIN_LEN = 512
D_FF = 2816
CONV_W = 3
N_MOD = 6

LANES = 128
VMEM_LIMIT = 56 * 1024 * 1024

ROW_TILE = 512
NSA_TILE = 256
FFN_CHUNK = 256
FFN_LOOKAHEAD = 2
HALO = 8
V_ROWS = 80


def _sigmoid(x):
    return 0.5 * jnp.tanh(0.5 * x) + 0.5


def _silu(x):
    h = 0.5 * x
    return h + h * jnp.tanh(h)


def _params(semantics):
    return pltpu.CompilerParams(dimension_semantics=semantics, vmem_limit_bytes=VMEM_LIMIT)


def _resident(shape):
    nd = len(shape)
    return pl.BlockSpec(tuple(shape), lambda *_: (0,) * nd, pipeline_mode=pl.Buffered(1))


def _lane_iota(shape):
    return lax.broadcasted_iota(jnp.int32, shape, len(shape) - 1)


def _row_iota(shape):
    return lax.broadcasted_iota(jnp.int32, shape, len(shape) - 2)


def _mod_kernel(c_ref, w_ref, b_ref, o_ref):
    sc = _silu(c_ref[...])
    o_ref[0] = jnp.dot(sc, w_ref[0], preferred_element_type=F32) + b_ref[0]


def _modulation(c, w_mod, b_mod):
    depth, d, n = w_mod.shape
    batch = c.shape[0]
    tn = 1024
    return pl.pallas_call(
        _mod_kernel,
        out_shape=jax.ShapeDtypeStruct((depth, batch, n), F32),
        grid=(depth, n // tn),
        in_specs=[
            pl.BlockSpec((batch, d), lambda l, j: (0, 0)),
            pl.BlockSpec((1, d, tn), lambda l, j: (l, 0, j)),
            pl.BlockSpec((1, 1, tn), lambda l, j: (l, 0, j)),
        ],
        out_specs=pl.BlockSpec((1, batch, tn), lambda l, j: (l, 0, j)),
        compiler_params=_params(("arbitrary", "arbitrary")),
        name="modulation",
    )(c, w_mod, b_mod.reshape(depth, 1, n))


def _norm_mod(x, g, scale, shift):
    ms = jnp.mean(x * x, axis=-1, keepdims=True)
    y = x * lax.rsqrt(ms + RMS_EPS) * g
    return y * (1.0 + scale) + shift


def _rope_chunk(xc, cos, sin_signed, first_half):
    partner = jnp.where(first_half, pltpu.roll(xc, LANES - HEAD_DIM // 2, 1),
                        pltpu.roll(xc, HEAD_DIM // 2, 1))
    return xc * cos + partner * sin_signed


_QKV_A = 3 * A_GROUP_WIDTH
_OFF_BQ = 3 * _QKV_A
_OFF_KV = _OFF_BQ + NSA_Q_WIDTH
_OFF_BLOG = _OFF_KV + 6 * LANES
_OFF_GA = _OFF_BLOG + LANES
N_IN_PACKED = _OFF_GA + 2 * 1024


def _inproj_kernel(x_ref, shift_ref, scale_ref, g_ref, cos_ref, sin_ref, w_ref,
                   qkv0_ref, qkv1_ref, qkv2_ref, bqr_ref, bqn_ref, kc_ref, vc_ref,
                   ksl_ref, vsl_ref, kwn_ref, vwn_ref, blog_ref, ga_ref, gb_ref, il_s):
    h = _norm_mod(x_ref[0], g_ref[...], scale_ref[...], shift_ref[...]).astype(BF16)
    cos = cos_ref[...]
    sin = sin_ref[...]
    rows = cos.shape[0]
    first_half = (_lane_iota((rows, LANES)) % HEAD_DIM) < (HEAD_DIM // 2)
    scale = HEAD_DIM ** -0.5

    def proj(off, width):
        return jnp.dot(h, w_ref[:, off:off + width], preferred_element_type=F32)

    def rope(y, mult):
        parts = [_rope_chunk(y[:, c:c + LANES], cos, sin, first_half) * mult
                 for c in range(0, y.shape[1], LANES)]
        return parts[0] if len(parts) == 1 else jnp.concatenate(parts, axis=1)

    def store_strided(out_ref, y, dil, col):
        nblk = y.shape[1] // LANES
        width = out_ref.shape[2] // dil
        for j in range(nblk):
            il_s[j] = y[:, j * LANES:(j + 1) * LANES]
        for r in range(dil):
            for j in range(nblk):
                c = r * width + col + j * LANES
                out_ref[0, :, c:c + LANES] = il_s[j, pl.ds(r, rows // dil, stride=dil), :].astype(out_ref.dtype)

    for (_, dil), out_ref, base in zip(DIL_GROUPS, (qkv0_ref, qkv1_ref, qkv2_ref), (0, _QKV_A, 2 * _QKV_A)):
        pieces = (rope(proj(base, 256), scale), rope(proj(base + 256, 256), 1.0), proj(base + 512, 256))
        for n, y in enumerate(pieces):
            if dil == 1:
                out_ref[0, :, n * 256:(n + 1) * 256] = y.astype(out_ref.dtype)
            else:
                store_strided(out_ref, y, dil, n * 256)

    bq = proj(_OFF_BQ, NSA_Q_WIDTH)
    bqn_ref[0] = (bq * scale).astype(bqn_ref.dtype)
    bqr_ref[0] = rope(bq, scale).astype(bqr_ref.dtype)

    kv = proj(_OFF_KV, 6 * LANES)
    store_strided(kc_ref, kv[:, 0:128], CMP_STRIDE, 0)
    store_strided(vc_ref, kv[:, 128:256], CMP_STRIDE, 0)
    ksl_ref[0] = rope(kv[:, 256:384], 1.0).astype(ksl_ref.dtype)
    vsl_ref[0] = kv[:, 384:512].astype(vsl_ref.dtype)
    kwn_ref[0] = rope(kv[:, 512:640], 1.0).astype(kwn_ref.dtype)
    vwn_ref[0] = kv[:, 640:768].astype(vwn_ref.dtype)

    blog_ref[0] = proj(_OFF_BLOG, LANES)
    ga_ref[0] = proj(_OFF_GA, 1024).astype(ga_ref.dtype)
    gb_ref[0] = proj(_OFF_GA + 1024, 1024).astype(gb_ref.dtype)


def _pack_w_in(w):
    aq, ak, av = w[:, 0:768], w[:, 768:1536], w[:, 1536:2304]
    cols = []
    for g in range(3):
        sl = slice(g * 256, (g + 1) * 256)
        cols += [aq[:, sl], ak[:, sl], av[:, sl]]
    cols.append(w[:, 2304:3584])
    cols.append(jnp.pad(w[:, 3584:3608], ((0, 0), (0, LANES - 24))))
    cols.append(w[:, 3608:5656])
    return jnp.concatenate(cols, axis=1).astype(BF16)


def _input_projection(x, mod_l, norm_g, cos_t, sin_t, w_packed):
    batch, seq, d = x.shape
    tm = ROW_TILE
    row = lambda w, dil=1: pl.BlockSpec((1, tm // dil, dil * w), lambda b, i: (b, i, 0))
    vec = lambda k: pl.BlockSpec((None, None, 1, d), lambda b, i, k=k: (b, k, 0, 0))
    shp = lambda w, dt, dil=1: jax.ShapeDtypeStruct((batch, seq // dil, dil * w), dt)
    dils = [dil for _, dil in DIL_GROUPS]
    return pl.pallas_call(
        _inproj_kernel,
        out_shape=[shp(_QKV_A, BF16, dil) for dil in dils] + [shp(512, BF16)] * 2
        + [shp(128, BF16, CMP_STRIDE)] * 2 + [shp(128, F32)] * 4 + [shp(128, F32)] + [shp(1024, BF16)] * 2,
        grid=(batch, seq // tm),
        in_specs=[
            row(d), vec(0), vec(1),
            pl.BlockSpec((1, d), lambda b, i: (0, 0)),
            pl.BlockSpec((tm, LANES), lambda b, i: (i, 0)),
            pl.BlockSpec((tm, LANES), lambda b, i: (i, 0)),
            _resident((d, N_IN_PACKED)),
        ],
        out_specs=[row(_QKV_A, dil) for dil in dils] + [row(512)] * 2 + [row(128, CMP_STRIDE)] * 2
        + [row(128)] * 5 + [row(1024)] * 2,
        scratch_shapes=[pltpu.VMEM((A_GROUP_WIDTH // LANES, tm, LANES), F32)],
        compiler_params=_params(("arbitrary", "arbitrary")),
        name="input_projection",
    )(x, mod_l, mod_l, norm_g.reshape(1, d), cos_t, sin_t, w_packed)


def _banded_kernel(qkv_ref, o_ref, lse_ref, *, length, nres):
    blk = 128
    nkeys = 2 * blk if length > blk else blk
    lane = _lane_iota((blk, LANES))
    lo = lane < HEAD_DIM
    qrow = _row_iota((2 * blk, nkeys)) % blk
    kcol = _lane_iota((2 * blk, nkeys))

    def unit(i, res, pair):
        base = res * _QKV_A + pair * LANES
        r0 = pl.multiple_of(i * blk, blk)
        qp = qkv_ref[0, pl.ds(r0, blk), base:base + LANES]
        if length > blk:
            ks = pl.multiple_of(jnp.maximum(r0 - blk, 0), blk)
        else:
            ks = 0
        k2 = qkv_ref[0, pl.ds(ks, nkeys), base + 256:base + 256 + LANES]
        v2 = qkv_ref[0, pl.ds(ks, nkeys), base + 512:base + 512 + LANES]
        zero = jnp.zeros_like(qp)
        qs = jnp.concatenate([jnp.where(lo, qp, zero), jnp.where(lo, zero, qp)], axis=0)
        s = lax.dot_general(qs, k2, (((1,), (1,)), ((), ())), preferred_element_type=F32)
        dist = (r0 + qrow) - (ks + kcol)
        s = jnp.where((dist >= 0) & (dist <= BAND_BACK), s, NEG_INF)
        m = jnp.max(s, axis=-1, keepdims=True)
        e = jnp.exp(s - m)
        den = jnp.sum(e, axis=-1, keepdims=True)
        o = jnp.dot(e.astype(BF16), v2, preferred_element_type=F32) * (1.0 / den)
        lse = m + jnp.log(den)
        ob = res * A_GROUP_WIDTH + pair * LANES
        o_ref[0, pl.ds(r0, blk), ob:ob + LANES] = jnp.where(lo, o[:blk], o[blk:]).astype(o_ref.dtype)
        lse_ref[0, pl.ds(r0, blk), ob:ob + LANES] = jnp.where(
            lo, jnp.broadcast_to(lse[:blk], (blk, LANES)), jnp.broadcast_to(lse[blk:], (blk, LANES)))

    def body(i, carry):
        for res in range(nres):
            for pair in range(2):
                unit(i, res, pair)
        return carry

    lax.fori_loop(0, length // blk, body, 0)


def _banded_attention(view, dil):
    batch, length, _ = view.shape
    nres = min(dil, 4)
    return pl.pallas_call(
        functools.partial(_banded_kernel, length=length, nres=nres),
        out_shape=[jax.ShapeDtypeStruct((batch, length, dil * A_GROUP_WIDTH), BF16),
                   jax.ShapeDtypeStruct((batch, length, dil * A_GROUP_WIDTH), F32)],
        grid=(batch, dil // nres),
        in_specs=[pl.BlockSpec((1, length, nres * _QKV_A), lambda b, r: (b, 0, r))],
        out_specs=[pl.BlockSpec((1, length, nres * A_GROUP_WIDTH), lambda b, r: (b, 0, r))] * 2,
        compiler_params=_params(("arbitrary", "arbitrary")),
        name=f"banded_attention_d{dil}",
    )(view)


def _compress_kernel(k_ref, v_ref, pe_ref, w1_ref, w2_ref, kc_ref, vc_ref):
    nchunk = k_ref.shape[1]
    for t, (src, dst) in enumerate(((k_ref, kc_ref), (v_ref, vc_ref))):
        c = src[0].astype(F32)
        top = jnp.dot((c + pe_ref[t, 0]).astype(BF16), w1_ref[t, 0], preferred_element_type=F32)
        bot = jnp.dot((c + pe_ref[t, 1]).astype(BF16), w1_ref[t, 1], preferred_element_type=F32)
        pre = top + pltpu.roll(bot, nchunk - 1, 0)
        for g in range(NSA_KV):
            hid = _silu(pre[:, g * CMP_HID:(g + 1) * CMP_HID]).astype(BF16)
            res = jnp.dot(hid, w2_ref[t], preferred_element_type=F32)
            dst[0, g] = (res if t == 0 else res.T).astype(dst.dtype)


def _pack_compress_weights(pe, w1, w2):
    half = CMP_LEN // 2
    w1r = w1.reshape(2, half, HEAD_DIM, CMP_HID)
    zero = jnp.zeros_like(w1r)
    w1p = jnp.stack([jnp.concatenate([w1r, zero], axis=-1), jnp.concatenate([zero, w1r], axis=-1)], axis=2)
    w1p = w1p.reshape(2, half * LANES, NSA_KV * CMP_HID)
    pep = jnp.tile(pe.reshape(2, half, 1, HEAD_DIM), (1, 1, NSA_KV, 1)).reshape(2, 1, half * LANES)
    w2p = jnp.concatenate([w2, w2], axis=1)
    return pep, w1p.astype(BF16), w2p.astype(BF16)


def _compress(kc_raw, vc_raw, pep, w1p, w2p):
    batch, nchunk, width = kc_raw.shape
    view = lambda a: a
    full = lambda a: _resident(a.shape)
    out = jax.ShapeDtypeStruct((batch, NSA_KV, nchunk, LANES), BF16)
    return pl.pallas_call(
        _compress_kernel,
        out_shape=[out, out],
        grid=(batch,),
        in_specs=[pl.BlockSpec((1, nchunk, width), lambda b: (b, 0, 0))] * 2
        + [full(pep), full(w1p), full(w2p)],
        out_specs=[pl.BlockSpec((1, NSA_KV, nchunk, LANES), lambda b: (b, 0, 0, 0))] * 2,
        compiler_params=_params(("arbitrary",)),
        name="nsa_compress",
    )(view(kc_raw), view(vc_raw), pep, w1p, w2p)


def _nsa_kernel(bqr_ref, bqn_ref, blog_ref, kcd_ref, vct_ref, ksl_ref, vsl_ref, kwn_ref, vwn_ref,
                ovl_ref, y_ref, ksel_s, vsel_s, kwin_s, vwin_s, m_s, acc_s, *, tile, seq):
    qi = pl.program_id(1)
    q0 = qi * tile
    cols = NSA_REP * tile
    nblk = seq // SEL_LEN
    ncmp = kcd_ref.shape[2]
    ntile = seq // tile
    nt = (((1,), (1,)), ((), ()))

    @pl.when(qi == 0)
    def _build_kv():
        lane = _lane_iota((seq, LANES))
        lo = lane < HEAD_DIM
        blk_of_row = _row_iota((seq, LANES)) // SEL_LEN
        selmask = jnp.where(blk_of_row == lane - HEAD_DIM, -MASK_BIG, 0.0)
        for src, dst, fill in ((ksl_ref, ksel_s, selmask), (kwn_ref, kwin_s, 0.0)):
            a = src[0]
            dst[0] = jnp.where(lo, a, fill).astype(dst.dtype)
            dst[1] = jnp.where(lo, pltpu.roll(a, HEAD_DIM, 1), fill).astype(dst.dtype)
        ones_rows = jnp.where(_row_iota((V_ROWS - HEAD_DIM, seq)) == 0, 1.0, 0.0)
        for src, dst in ((vsl_ref, vsel_s), (vwn_ref, vwin_s)):
            vt = src[0].T
            for g in range(NSA_KV):
                full = jnp.concatenate([vt[g * HEAD_DIM:(g + 1) * HEAD_DIM], ones_rows], axis=0).astype(dst.dtype)
                for kt in range(ntile):
                    dst[g, kt] = full[:, kt * tile:(kt + 1) * tile]

    lo_t = _lane_iota((tile, LANES)) < HEAD_DIM

    def stack_heads(q_ref, g, extra):
        out = []
        for pair in range(2):
            c = g * 256 + pair * LANES
            qp = q_ref[0, :, c:c + LANES].astype(F32)
            out.append(jnp.where(lo_t, qp, extra))
            out.append(jnp.where(lo_t, pltpu.roll(qp, HEAD_DIM, 1), extra))
        return jnp.concatenate(out, axis=0).astype(BF16)

    qpos = q0 + (_lane_iota((tile, cols)) % tile)
    krow = _row_iota((tile, cols))

    def flash_step(qs, k_s, v_s, g, kt, mask):
        k0 = pl.multiple_of(kt * tile, tile)
        s = lax.dot_general(k_s[g, pl.ds(k0, tile), :], qs, nt, preferred_element_type=F32)
        if mask == "causal":
            s = jnp.where(qpos >= k0 + krow, s, NEG_INF)
        elif mask == "band":
            s = jnp.where(qpos - (k0 + krow) <= WIN_LEN - 1, s, NEG_INF)
        m_old = m_s[...]
        m_new = jnp.maximum(m_old, jnp.max(s, axis=0, keepdims=True))
        alpha = jnp.exp(m_old - m_new)
        p = jnp.exp(s - m_new).astype(BF16)
        acc_s[...] = alpha * acc_s[...] + jnp.dot(v_s[g, kt], p, preferred_element_type=F32)
        m_s[...] = m_new

    def flash_init():
        m_s[...] = jnp.full(m_s.shape, NEG_INF, F32)
        acc_s[...] = jnp.zeros(acc_s.shape, F32)

    def flash_result():
        acc = acc_s[...]
        return acc[0:HEAD_DIM] * (1.0 / acc[HEAD_DIM:HEAD_DIM + 1])

    gates = _sigmoid(blog_ref[0].T)
    zero_t = jnp.zeros((tile, LANES), F32)
    blk = _row_iota((nblk, tile))
    tpos = q0 + _lane_iota((nblk, tile))
    heads = []

    for g in range(NSA_KV):
        qn = stack_heads(bqn_ref, g, zero_t)
        s = lax.dot_general(kcd_ref[0, g], qn, nt, preferred_element_type=F32)
        tq = q0 + (_lane_iota((ncmp, cols)) % tile)
        cvis = (_row_iota((ncmp, cols)) * CMP_STRIDE + (CMP_LEN - 1)) <= tq
        s = jnp.where(cvis, s, NEG_INF)
        m = jnp.max(s, axis=0, keepdims=True)
        e = jnp.where(cvis, jnp.exp(s - m), 0.0)
        den = jnp.sum(e, axis=0, keepdims=True)
        p = e * jnp.where(den > 0.0, 1.0 / den, 0.0)
        o_cmp = jnp.dot(vct_ref[0, g, 0:HEAD_DIM, :], p.astype(BF16), preferred_element_type=F32)

        psum = p[:, 0:tile] + p[:, tile:2 * tile] + p[:, 2 * tile:3 * tile] + p[:, 3 * tile:4 * tile]
        p_hi = psum.astype(BF16)
        p_lo = (psum - p_hi.astype(F32)).astype(BF16)
        ovl = ovl_ref[...]
        imp = (jnp.dot(ovl, p_hi, preferred_element_type=F32)
               + jnp.dot(ovl, p_lo, preferred_element_type=F32))
        cur = tpos // SEL_LEN
        forced = (blk == 0) | (blk == cur) | (blk == cur - 1)
        imp = jnp.where(forced, FORCE_SCORE, imp)
        imp = jnp.where(blk * SEL_LEN <= tpos, imp, NEG_INF)
        rank = jnp.zeros((nblk, tile), jnp.int32)
        for k in range(nblk):
            vk = imp[k:k + 1, :]
            ahead = (vk > imp) | ((vk == imp) & (blk > k))
            rank = rank + ahead.astype(jnp.int32)
        notsel = jnp.where(rank >= SEL_TOPK, 1.0, 0.0)
        extra = jnp.concatenate([jnp.zeros((HEAD_DIM, tile), F32), notsel,
                                 jnp.zeros((LANES - HEAD_DIM - nblk, tile), F32)], axis=0).T

        qs = stack_heads(bqr_ref, g, extra)
        flash_init()
        lax.fori_loop(0, qi, lambda kt, c: (flash_step(qs, ksel_s, vsel_s, g, kt, None), c)[1], 0)
        flash_step(qs, ksel_s, vsel_s, g, qi, "causal")
        o_sel = flash_result()

        qw = stack_heads(bqr_ref, g, zero_t)
        flash_init()

        @pl.when(qi >= 2)
        def _():
            flash_step(qw, kwin_s, vwin_s, g, qi - 2, "band")

        @pl.when(qi >= 1)
        def _():
            flash_step(qw, kwin_s, vwin_s, g, qi - 1, None)

        flash_step(qw, kwin_s, vwin_s, g, qi, "causal")
        o_win = flash_result()

        for r in range(NSA_REP):
            c = (g * NSA_REP + r) * 3
            sl = slice(r * tile, (r + 1) * tile)
            heads.append(gates[c:c + 1] * o_cmp[:, sl] + gates[c + 1:c + 2] * o_sel[:, sl]
                         + gates[c + 2:c + 3] * o_win[:, sl])

    y_ref[0] = jnp.concatenate(heads, axis=0).T.astype(y_ref.dtype)


def _overlap_matrix(seq):
    starts = np.arange(seq // CMP_STRIDE) * CMP_STRIDE
    bstart = np.arange(seq // SEL_LEN) * SEL_LEN
    ovl = (starts[None, :] < bstart[:, None] + SEL_LEN) & (starts[None, :] + CMP_LEN > bstart[:, None])
    return jnp.asarray(ovl, dtype=BF16)


def _nsa_attention(bqr, bqn, blog, kcd, vct, ksl, vsl, kwn, vwn):
    batch, seq, _ = bqr.shape
    tile = NSA_TILE
    assert 2 * tile == WIN_LEN
    cols = NSA_REP * tile
    ncmp = seq // CMP_STRIDE
    nblk = seq // SEL_LEN
    qspec = lambda w: pl.BlockSpec((1, tile, w), lambda b, i: (b, i, 0))
    kvspec = pl.BlockSpec((1, seq, LANES), lambda b, i: (b, 0, 0))
    cspec = pl.BlockSpec((1, NSA_KV, ncmp, LANES), lambda b, i: (b, 0, 0, 0))
    k_scratch = pltpu.VMEM((NSA_KV, seq, LANES), BF16)
    v_scratch = pltpu.VMEM((NSA_KV, seq // tile, V_ROWS, tile), BF16)
    return pl.pallas_call(
        functools.partial(_nsa_kernel, tile=tile, seq=seq),
        out_shape=jax.ShapeDtypeStruct((batch, seq, NSA_Q_WIDTH), BF16),
        grid=(batch, seq // tile),
        in_specs=[qspec(NSA_Q_WIDTH), qspec(NSA_Q_WIDTH), qspec(LANES), cspec, cspec,
                  kvspec, kvspec, kvspec, kvspec,
                  pl.BlockSpec((nblk, ncmp), lambda b, i: (0, 0))],
        out_specs=qspec(NSA_Q_WIDTH),
        scratch_shapes=[k_scratch, v_scratch, k_scratch, v_scratch,
                        pltpu.VMEM((1, cols), F32), pltpu.VMEM((V_ROWS, cols), F32)],
        compiler_params=_params(("arbitrary", "arbitrary")),
        name="nsa_attention",
    )(bqr, bqn, blog, kcd, vct, ksl, vsl, kwn, vwn, _overlap_matrix(seq))


def _merge_kernel(o0_ref, o1_ref, o2_ref, l0_ref, l1_ref, l2_ref, yb_ref, ga_ref, gb_ref, x_ref,
                  gate_ref, wa_ref, wb_ref, wo_ref, out_ref, il_s):
    tm = x_ref.shape[1]

    def natural(ref, dil, slot):
        if dil == 1:
            return ref[0].astype(F32)
        nblk = A_GROUP_WIDTH // LANES
        for r in range(dil):
            for j in range(nblk):
                c = r * A_GROUP_WIDTH + j * LANES
                il_s[slot, j, pl.ds(r, tm // dil, stride=dil), :] = ref[0, :, c:c + LANES].astype(F32)
        return jnp.concatenate([il_s[slot, j] for j in range(nblk)], axis=1)

    dils = [dil for _, dil in DIL_GROUPS]
    o0, o1, o2 = (natural(r, dil, n) for n, (r, dil) in enumerate(zip((o0_ref, o1_ref, o2_ref), dils)))
    l0, l1, l2 = (natural(r, dil, 3 + n) for n, (r, dil) in enumerate(zip((l0_ref, l1_ref, l2_ref), dils)))
    m = jnp.maximum(jnp.maximum(l0, l1), l2)
    e0, e1, e2 = jnp.exp(l0 - m), jnp.exp(l1 - m), jnp.exp(l2 - m)
    inv = 1.0 / (e0 + e1 + e2)
    y_a = (e0 * inv) * o0 + (e1 * inv) * o1 + (e2 * inv) * o2
    pa = jnp.dot(y_a.astype(BF16), wa_ref[...], preferred_element_type=F32)
    pb = jnp.dot(yb_ref[0], wb_ref[...], preferred_element_type=F32)
    merged = _sigmoid(ga_ref[0].astype(F32)) * pa + _sigmoid(gb_ref[0].astype(F32)) * pb
    out = jnp.dot(merged.astype(BF16), wo_ref[...], preferred_element_type=F32)
    out_ref[0] = x_ref[0] + gate_ref[...] * out


def _merge(o_lse, y_b, ga, gb, x, mod_l, w_a, w_b, w_o):
    batch, seq, d = x.shape
    tm = ROW_TILE
    row = lambda w, dil=1: pl.BlockSpec((1, tm // dil, dil * w), lambda b, i: (b, i, 0))
    full = lambda a: _resident(a.shape)
    (o0, l0), (o1, l1), (o2, l2) = o_lse
    grp = [row(A_GROUP_WIDTH, dil) for _, dil in DIL_GROUPS]
    return pl.pallas_call(
        _merge_kernel,
        out_shape=jax.ShapeDtypeStruct(x.shape, F32),
        grid=(batch, seq // tm),
        in_specs=grp + grp + [row(NSA_Q_WIDTH), row(d), row(d), row(d),
                              pl.BlockSpec((None, None, 1, d), lambda b, i: (b, 2, 0, 0)),
                              full(w_a), full(w_b), full(w_o)],
        out_specs=row(d),
        scratch_shapes=[pltpu.VMEM((6, A_GROUP_WIDTH // LANES, tm, LANES), F32)],
        compiler_params=_params(("arbitrary", "arbitrary")),
        name="branch_merge",
    )(o0, o1, o2, l0, l1, l2, y_b, ga, gb, x, mod_l, w_a, w_b, w_o)


def _ffn_kernel(x_ref, halo_ref, shift_ref, scale_ref, gate_ref, g_ref, wup_ref, cw_ref, cb_ref,
                wdn_ref, fin_ref, out_ref, he_s, *, final):
    i = pl.program_id(1)
    tm = x_ref.shape[1]
    hh = _norm_mod(halo_ref[0], g_ref[...], scale_ref[...], shift_ref[...]) * jnp.where(i > 0, 1.0, 0.0)
    he_s[0:tm] = _norm_mod(x_ref[0], g_ref[...], scale_ref[...], shift_ref[...]).astype(BF16)
    he_s[tm:] = hh.astype(BF16)

    def up(col):
        return jnp.dot(he_s[...], wup_ref[:, col:col + FFN_CHUNK], preferred_element_type=F32)

    def conv(u, col):
        w = cw_ref[:, col:col + FFN_CHUNK]
        return (w[0:1] * pltpu.roll(u, 2, 0)[:tm] + w[1:2] * pltpu.roll(u, 1, 0)[:tm]
                + (w[2:3] * u[:tm] + cb_ref[:, col:col + FFN_CHUNK]))

    chunks = list(range(0, D_FF, FFN_CHUNK))
    pending = [(up(c), up(D_FF + c)) for c in chunks[:FFN_LOOKAHEAD]]
    for n, c in enumerate(chunks):
        ug, uv = pending.pop(0)
        if n + FFN_LOOKAHEAD < len(chunks):
            ahead = chunks[n + FFN_LOOKAHEAD]
            pending.append((up(ahead), up(D_FF + ahead)))
        act = (_silu(conv(ug, c)) * conv(uv, D_FF + c)).astype(BF16)
        part = jnp.dot(act, wdn_ref[c:c + FFN_CHUNK, :], preferred_element_type=F32)
        if n == 0:
            out_ref[0] = part
        else:
            out_ref[0] += part
    y = x_ref[0] + gate_ref[...] * out_ref[0]
    if final:
        ms = jnp.mean(y * y, axis=-1, keepdims=True)
        y = y * lax.rsqrt(ms + RMS_EPS) * fin_ref[...]
    out_ref[0] = y


def _ffn(x, mod_l, norm_g, w_up, conv_w, conv_b, w_down, final_g, final):
    batch, seq, d = x.shape
    tm = ROW_TILE
    row = pl.BlockSpec((1, tm, d), lambda b, i: (b, i, 0))
    halo = pl.BlockSpec((1, HALO, d), lambda b, i: (b, jnp.maximum(i * (tm // HALO) - 1, 0), 0))
    vec = lambda k: pl.BlockSpec((None, None, 1, d), lambda b, i, k=k: (b, k, 0, 0))
    full = lambda a: _resident(a.shape)
    norm_g = norm_g.reshape(1, d)
    conv_b = conv_b.reshape(1, -1)
    final_g = final_g.reshape(1, d)
    return pl.pallas_call(
        functools.partial(_ffn_kernel, final=final),
        out_shape=jax.ShapeDtypeStruct(x.shape, F32),
        grid=(batch, seq // tm),
        in_specs=[row, halo, vec(3), vec(4), vec(5), full(norm_g), full(w_up), full(conv_w),
                  full(conv_b), full(w_down), full(final_g)],
        out_specs=row,
        scratch_shapes=[pltpu.VMEM((tm + HALO, d), BF16)],
        compiler_params=_params(("arbitrary", "arbitrary")),
        name="conv_ffn",
    )(x, x, mod_l, mod_l, mod_l, norm_g, w_up, conv_w, conv_b, w_down, final_g)


def _rope_tables(seq):
    inv = 1.0 / (ROPE_THETA ** (jnp.arange(0, HEAD_DIM, 2, dtype=F32) / HEAD_DIM))
    ang = jnp.arange(seq, dtype=F32)[:, None] * inv[None, :]
    cos, sin = jnp.cos(ang), jnp.sin(ang)
    cos_t = jnp.tile(cos, (1, LANES // (HEAD_DIM // 2)))
    sin_t = jnp.tile(jnp.concatenate([-sin, sin], axis=1), (1, LANES // HEAD_DIM))
    return cos_t, sin_t


def kernel(x, c, norm1_g, norm2_g, final_g, w_mod, b_mod, w_in, cmp_pe_k, cmp_pe_v, cmp_w1_k, cmp_w2_k, cmp_w1_v, cmp_w2_v, w_br_a, w_br_b, w_out, w_up, conv_w, conv_b, w_down):
    batch, seq, d = x.shape
    depth = w_in.shape[0]
    cos_t, sin_t = _rope_tables(seq)
    mod = _modulation(c, w_mod, b_mod).reshape(depth, batch, N_MOD, 1, d)
    for layer in range(depth):
        mod_l = mod[layer]
        (qkv0, qkv1, qkv2, bqr, bqn, kc_raw, vc_raw, ksl, vsl, kwn, vwn, blog, ga, gb) = _input_projection(
            x, mod_l, norm1_g[layer], cos_t, sin_t, _pack_w_in(w_in[layer]))
        o_lse = [_banded_attention(qkv, dil) for qkv, (_, dil) in zip((qkv0, qkv1, qkv2), DIL_GROUPS)]

        pk = _pack_compress_weights(cmp_pe_k[layer], cmp_w1_k[layer], cmp_w2_k[layer])
        pv = _pack_compress_weights(cmp_pe_v[layer], cmp_w1_v[layer], cmp_w2_v[layer])
        pep, w1p, w2p = (jnp.stack([a, b]) for a, b in zip(pk, pv))
        kcd, vcd = _compress(kc_raw, vc_raw, pep, w1p, w2p)
        y_b = _nsa_attention(bqr, bqn, blog, kcd, vcd, ksl, vsl, kwn, vwn)

        x = _merge(o_lse, y_b, ga, gb, x, mod_l, w_br_a[layer].astype(BF16),
                   w_br_b[layer].astype(BF16), w_out[layer].astype(BF16))
        x = _ffn(x, mod_l, norm2_g[layer], w_up[layer].astype(BF16), conv_w[layer], conv_b[layer],
                 w_down[layer].astype(BF16), final_g, final=(layer == depth - 1))
    return x
```

```python
import functools

import jax
import jax.numpy as jnp
import numpy as np
from jax import lax
from jax.experimental import pallas as pl
from jax.experimental.pallas import tpu as pltpu

F32 = jnp.float32
BF16 = jnp.bfloat16

HEAD_DIM = 64
ROPE_THETA = 10000.0
RMS_EPS = 1e-6
NEG_INF = -1e30
FORCE_SCORE = 1e9
MASK_BIG = float(2.0 ** 100)
LOG2E = 1.4426950408889634
LN2 = 0.6931471805599453

DIL_GROUPS = ((128, 1), (512, 4), (2048, 16))
A_HEADS = 4
A_GROUP_WIDTH = A_HEADS * HEAD_DIM
BAND_BACK = 128
NSA_HEADS = 8
NSA_KV = 2
NSA_REP = NSA_HEADS // NSA_KV
NSA_Q_WIDTH = NSA_HEADS * HEAD_DIM
CMP_LEN = 32
CMP_STRIDE = 16
CMP_HID = 256
SEL_LEN = 64
SEL_TOPK = 16
WIN_LEN = 512
D_FF = 2816
CONV_W = 3
N_MOD = 6

LANES = 128
VMEM_LIMIT = 56 * 1024 * 1024

ROW_TILE = 512
NSA_TILE = 256
FFN_CHUNK = 256
FFN_LOOKAHEAD = 2
HALO = 8
V_ROWS = 80


def _sigmoid(x):
    return 0.5 * jnp.tanh(0.5 * x) + 0.5


def _silu(x):
    h = 0.5 * x
    return h + h * jnp.tanh(h)


def _params(semantics):
    return pltpu.CompilerParams(dimension_semantics=semantics, vmem_limit_bytes=VMEM_LIMIT)


def _resident(shape):
    nd = len(shape)
    return pl.BlockSpec(tuple(shape), lambda *_: (0,) * nd, pipeline_mode=pl.Buffered(1))


def _lane_iota(shape):
    return lax.broadcasted_iota(jnp.int32, shape, len(shape) - 1)


def _row_iota(shape):
    return lax.broadcasted_iota(jnp.int32, shape, len(shape) - 2)


def _mod_kernel(c_ref, w_ref, b_ref, o_ref):
    sc = _silu(c_ref[...])
    o_ref[0] = jnp.dot(sc, w_ref[0], preferred_element_type=F32) + b_ref[0]


def _modulation(c, w_mod, b_mod):
    depth, d, n = w_mod.shape
    batch = c.shape[0]
    tn = 1024
    return pl.pallas_call(
        _mod_kernel,
        out_shape=jax.ShapeDtypeStruct((depth, batch, n), F32),
        grid=(depth, n // tn),
        in_specs=[
            pl.BlockSpec((batch, d), lambda l, j: (0, 0)),
            pl.BlockSpec((1, d, tn), lambda l, j: (l, 0, j)),
            pl.BlockSpec((1, 1, tn), lambda l, j: (l, 0, j)),
        ],
        out_specs=pl.BlockSpec((1, batch, tn), lambda l, j: (l, 0, j)),
        compiler_params=_params(("arbitrary", "arbitrary")),
        name="modulation",
    )(c, w_mod, b_mod.reshape(depth, 1, n))


def _norm_mod(x, g, scale, shift):
    ms = jnp.mean(x * x, axis=-1, keepdims=True)
    y = x * lax.rsqrt(ms + RMS_EPS) * g
    return y * (1.0 + scale) + shift


def _rope_chunk(xc, cos, sin_signed, first_half):
    partner = jnp.where(first_half, pltpu.roll(xc, LANES - HEAD_DIM // 2, 1),
                        pltpu.roll(xc, HEAD_DIM // 2, 1))
    return xc * cos + partner * sin_signed


_QKV_A = 3 * A_GROUP_WIDTH
_OFF_BQ = 3 * _QKV_A
_OFF_KV = _OFF_BQ + NSA_Q_WIDTH
_OFF_BLOG = _OFF_KV + 6 * LANES
_OFF_GA = _OFF_BLOG + LANES
N_IN_PACKED = _OFF_GA + 2 * 1024


def _inproj_kernel(x_ref, shift_ref, scale_ref, g_ref, cos_ref, sin_ref, w_ref,
                   qkv0_ref, qkv1_ref, qkv2_ref, bqr_ref, bqn_ref, kc_ref, vc_ref,
                   ksl_ref, vsl_ref, kwn_ref, vwn_ref, blog_ref, ga_ref, gb_ref, il_s):
    h = _norm_mod(x_ref[0], g_ref[...], scale_ref[...], shift_ref[...]).astype(BF16)
    cos = cos_ref[...]
    sin = sin_ref[...]
    rows = cos.shape[0]
    first_half = (_lane_iota((rows, LANES)) % HEAD_DIM) < (HEAD_DIM // 2)
    scale = HEAD_DIM ** -0.5 * LOG2E

    def proj(off, width):
        return jnp.dot(h, w_ref[:, off:off + width], preferred_element_type=F32)

    def rope(y, mult):
        parts = [_rope_chunk(y[:, c:c + LANES], cos, sin, first_half) * mult
                 for c in range(0, y.shape[1], LANES)]
        return parts[0] if len(parts) == 1 else jnp.concatenate(parts, axis=1)

    def store_strided(out_ref, y, dil, col):
        nblk = y.shape[1] // LANES
        width = out_ref.shape[2] // dil
        for j in range(nblk):
            il_s[j] = y[:, j * LANES:(j + 1) * LANES]
        for r in range(dil):
            for j in range(nblk):
                c = r * width + col + j * LANES
                out_ref[0, :, c:c + LANES] = il_s[j, pl.ds(r, rows // dil, stride=dil), :].astype(out_ref.dtype)

    for (_, dil), out_ref, base in zip(DIL_GROUPS, (qkv0_ref, qkv1_ref, qkv2_ref), (0, _QKV_A, 2 * _QKV_A)):
        pieces = (rope(proj(base, 256), scale), rope(proj(base + 256, 256), 1.0), proj(base + 512, 256))
        for n, y in enumerate(pieces):
            if dil == 1:
                out_ref[0, :, n * 256:(n + 1) * 256] = y.astype(out_ref.dtype)
            else:
                store_strided(out_ref, y, dil, n * 256)

    bq = proj(_OFF_BQ, NSA_Q_WIDTH)
    bqn_ref[0] = (bq * scale).astype(bqn_ref.dtype)
    bqr_ref[0] = rope(bq, scale).astype(bqr_ref.dtype)

    kv = proj(_OFF_KV, 6 * LANES)
    store_strided(kc_ref, kv[:, 0:128], CMP_STRIDE, 0)
    store_strided(vc_ref, kv[:, 128:256], CMP_STRIDE, 0)
    ksl_ref[0] = rope(kv[:, 256:384], 1.0).astype(ksl_ref.dtype)
    vsl_ref[0] = kv[:, 384:512].astype(vsl_ref.dtype)
    kwn_ref[0] = rope(kv[:, 512:640], 1.0).astype(kwn_ref.dtype)
    vwn_ref[0] = kv[:, 640:768].astype(vwn_ref.dtype)

    blog_ref[0] = proj(_OFF_BLOG, LANES)
    ga_ref[0] = proj(_OFF_GA, 1024).astype(ga_ref.dtype)
    gb_ref[0] = proj(_OFF_GA + 1024, 1024).astype(gb_ref.dtype)


def _pack_w_in(w):
    aq, ak, av = w[:, 0:768], w[:, 768:1536], w[:, 1536:2304]
    cols = []
    for g in range(3):
        sl = slice(g * 256, (g + 1) * 256)
        cols += [aq[:, sl], ak[:, sl], av[:, sl]]
    cols.append(w[:, 2304:3584])
    cols.append(jnp.pad(w[:, 3584:3608], ((0, 0), (0, LANES - 24))))
    cols.append(w[:, 3608:5656])
    return jnp.concatenate(cols, axis=1).astype(BF16)


def _input_projection(x, mod_l, norm_g, cos_t, sin_t, w_packed):
    batch, seq, d = x.shape
    tm = ROW_TILE
    row = lambda w, dil=1: pl.BlockSpec((1, tm // dil, dil * w), lambda b, i: (b, i, 0))
    vec = lambda k: pl.BlockSpec((None, None, 1, d), lambda b, i, k=k: (b, k, 0, 0))
    shp = lambda w, dt, dil=1: jax.ShapeDtypeStruct((batch, seq // dil, dil * w), dt)
    dils = [dil for _, dil in DIL_GROUPS]
    return pl.pallas_call(
        _inproj_kernel,
        out_shape=[shp(_QKV_A, BF16, dil) for dil in dils] + [shp(512, BF16)] * 2
        + [shp(128, BF16, CMP_STRIDE)] * 2 + [shp(128, F32)] * 4 + [shp(128, F32)] + [shp(1024, BF16)] * 2,
        grid=(batch, seq // tm),
        in_specs=[
            row(d), vec(0), vec(1),
            pl.BlockSpec((1, d), lambda b, i: (0, 0)),
            pl.BlockSpec((tm, LANES), lambda b, i: (i, 0)),
            pl.BlockSpec((tm, LANES), lambda b, i: (i, 0)),
            _resident((d, N_IN_PACKED)),
        ],
        out_specs=[row(_QKV_A, dil) for dil in dils] + [row(512)] * 2 + [row(128, CMP_STRIDE)] * 2
        + [row(128)] * 5 + [row(1024)] * 2,
        scratch_shapes=[pltpu.VMEM((A_GROUP_WIDTH // LANES, tm, LANES), F32)],
        compiler_params=_params(("arbitrary", "arbitrary")),
        name="input_projection",
    )(x, mod_l, mod_l, norm_g.reshape(1, d), cos_t, sin_t, w_packed)


def _banded_kernel(qkv_ref, o_ref, lse_ref, *, length, nres):
    blk = 128
    nkeys = 2 * blk if length > blk else blk
    lane = _lane_iota((blk, LANES))
    lo = lane < HEAD_DIM
    qrow = _row_iota((2 * blk, nkeys)) % blk
    kcol = _lane_iota((2 * blk, nkeys))

    def unit(i, res, pair):
        base = res * _QKV_A + pair * LANES
        r0 = pl.multiple_of(i * blk, blk)
        qp = qkv_ref[0, pl.ds(r0, blk), base:base + LANES]
        if length > blk:
            ks = pl.multiple_of(jnp.maximum(r0 - blk, 0), blk)
        else:
            ks = 0
        k2 = qkv_ref[0, pl.ds(ks, nkeys), base + 256:base + 256 + LANES]
        v2 = qkv_ref[0, pl.ds(ks, nkeys), base + 512:base + 512 + LANES]
        zero = jnp.zeros_like(qp)
        qs = jnp.concatenate([jnp.where(lo, qp, zero), jnp.where(lo, zero, qp)], axis=0)
        s = lax.dot_general(qs, k2, (((1,), (1,)), ((), ())), preferred_element_type=F32)
        dist = (r0 + qrow) - (ks + kcol)
        s = jnp.where((dist >= 0) & (dist <= BAND_BACK), s, NEG_INF)
        m = jnp.max(s, axis=-1, keepdims=True)
        e = jnp.exp2(s - m)
        den = jnp.sum(e, axis=-1, keepdims=True)
        o = jnp.dot(e.astype(BF16), v2, preferred_element_type=F32) * (1.0 / den)
        lse = m * LN2 + jnp.log(den)
        ob = res * A_GROUP_WIDTH + pair * LANES
        o_ref[0, pl.ds(r0, blk), ob:ob + LANES] = jnp.where(lo, o[:blk], o[blk:]).astype(o_ref.dtype)
        lse_ref[0, pl.ds(r0, blk), ob:ob + LANES] = jnp.where(
            lo, jnp.broadcast_to(lse[:blk], (blk, LANES)), jnp.broadcast_to(lse[blk:], (blk, LANES)))

    def body(i, carry):
        for res in range(nres):
            for pair in range(2):
                unit(i, res, pair)
        return carry

    lax.fori_loop(0, length // blk, body, 0)


def _banded_attention(view, dil):
    batch, length, _ = view.shape
    nres = min(dil, 4)
    return pl.pallas_call(
        functools.partial(_banded_kernel, length=length, nres=nres),
        out_shape=[jax.ShapeDtypeStruct((batch, length, dil * A_GROUP_WIDTH), BF16),
                   jax.ShapeDtypeStruct((batch, length, dil * A_GROUP_WIDTH), F32)],
        grid=(batch, dil // nres),
        in_specs=[pl.BlockSpec((1, length, nres * _QKV_A), lambda b, r: (b, 0, r))],
        out_specs=[pl.BlockSpec((1, length, nres * A_GROUP_WIDTH), lambda b, r: (b, 0, r))] * 2,
        compiler_params=_params(("arbitrary", "arbitrary")),
        name=f"banded_attention_d{dil}",
    )(view)


def _compress_kernel(k_ref, v_ref, pe_ref, w1_ref, w2_ref, kc_ref, vc_ref):
    nchunk = k_ref.shape[1]
    for t, (src, dst) in enumerate(((k_ref, kc_ref), (v_ref, vc_ref))):
        c = src[0].astype(F32)
        top = jnp.dot((c + pe_ref[t, 0]).astype(BF16), w1_ref[t, 0], preferred_element_type=F32)
        bot = jnp.dot((c + pe_ref[t, 1]).astype(BF16), w1_ref[t, 1], preferred_element_type=F32)
        pre = top + pltpu.roll(bot, nchunk - 1, 0)
        for g in range(NSA_KV):
            hid = _silu(pre[:, g * CMP_HID:(g + 1) * CMP_HID]).astype(BF16)
            res = jnp.dot(hid, w2_ref[t], preferred_element_type=F32)
            dst[0, g] = (res if t == 0 else res.T).astype(dst.dtype)


def _pack_compress_weights(pe, w1, w2):
    half = CMP_LEN // 2
    w1r = w1.reshape(2, half, HEAD_DIM, CMP_HID)
    zero = jnp.zeros_like(w1r)
    w1p = jnp.stack([jnp.concatenate([w1r, zero], axis=-1), jnp.concatenate([zero, w1r], axis=-1)], axis=2)
    w1p = w1p.reshape(2, half * LANES, NSA_KV * CMP_HID)
    pep = jnp.tile(pe.reshape(2, half, 1, HEAD_DIM), (1, 1, NSA_KV, 1)).reshape(2, 1, half * LANES)
    w2p = jnp.concatenate([w2, w2], axis=1)
    return pep, w1p.astype(BF16), w2p.astype(BF16)


def _compress(kc_raw, vc_raw, pep, w1p, w2p):
    batch, nchunk, width = kc_raw.shape
    view = lambda a: a
    full = lambda a: _resident(a.shape)
    out = jax.ShapeDtypeStruct((batch, NSA_KV, nchunk, LANES), BF16)
    return pl.pallas_call(
        _compress_kernel,
        out_shape=[out, out],
        grid=(batch,),
        in_specs=[pl.BlockSpec((1, nchunk, width), lambda b: (b, 0, 0))] * 2
        + [full(pep), full(w1p), full(w2p)],
        out_specs=[pl.BlockSpec((1, NSA_KV, nchunk, LANES), lambda b: (b, 0, 0, 0))] * 2,
        compiler_params=_params(("arbitrary",)),
        name="nsa_compress",
    )(view(kc_raw), view(vc_raw), pep, w1p, w2p)


def _nsa_kernel(bqr_ref, bqn_ref, blog_ref, kcd_ref, vct_ref, ksl_ref, vsl_ref, kwn_ref, vwn_ref,
                ovl_ref, y_ref, ksel_s, vsel_s, kwin_s, vwin_s, m_s, acc_s, alpha_s, s_s, p_s, *, tile, seq):
    qi = pl.program_id(1)
    q0 = qi * tile
    cols = NSA_REP * tile
    nblk = seq // SEL_LEN
    ncmp = kcd_ref.shape[2]
    ntile = seq // tile
    nt = (((1,), (1,)), ((), ()))

    @pl.when(qi == 0)
    def _build_kv():
        lane = _lane_iota((seq, LANES))
        lo = lane < HEAD_DIM
        blk_of_row = _row_iota((seq, LANES)) // SEL_LEN
        selmask = jnp.where(blk_of_row == lane - HEAD_DIM, -MASK_BIG, 0.0)
        for src, dst, fill in ((ksl_ref, ksel_s, selmask), (kwn_ref, kwin_s, 0.0)):
            a = src[0]
            dst[0] = jnp.where(lo, a, fill).astype(dst.dtype)
            dst[1] = jnp.where(lo, pltpu.roll(a, HEAD_DIM, 1), fill).astype(dst.dtype)
        ones_rows = jnp.where(_row_iota((V_ROWS - HEAD_DIM, seq)) == 0, 1.0, 0.0)
        for src, dst in ((vsl_ref, vsel_s), (vwn_ref, vwin_s)):
            vt = src[0].T
            for g in range(NSA_KV):
                full = jnp.concatenate([vt[g * HEAD_DIM:(g + 1) * HEAD_DIM], ones_rows], axis=0).astype(dst.dtype)
                for kt in range(ntile):
                    dst[g, kt] = full[:, kt * tile:(kt + 1) * tile]

    lo_t = _lane_iota((tile, LANES)) < HEAD_DIM

    def stack_heads(q_ref, g, extra):
        out = []
        for pair in range(2):
            c = g * 256 + pair * LANES
            qp = q_ref[0, :, c:c + LANES].astype(F32)
            out.append(jnp.where(lo_t, qp, extra))
            out.append(jnp.where(lo_t, pltpu.roll(qp, HEAD_DIM, 1), extra))
        return jnp.concatenate(out, axis=0).astype(BF16)

    qpos = q0 + (_lane_iota((tile, cols)) % tile)
    krow = _row_iota((tile, cols))

    def scores(k_s, g, kt, qs, slot):
        k0 = pl.multiple_of(kt * tile, tile)
        s_s[g, slot] = lax.dot_general(k_s[g, pl.ds(k0, tile), :], qs, nt, preferred_element_type=F32)

    def softmax(g, slot, keep):
        s = s_s[g, slot]
        if keep is not None:
            s = jnp.where(keep, s, NEG_INF)
        m_old = m_s[g]
        m_new = jnp.maximum(m_old, jnp.max(s, axis=0, keepdims=True))
        alpha_s[g, slot] = jnp.exp2(m_old - m_new)
        p_s[g, slot] = jnp.exp2(s - m_new).astype(BF16)
        m_s[g] = m_new

    def pv(v_s, g, kt, slot):
        acc_s[g] = alpha_s[g, slot] * acc_s[g] + jnp.dot(v_s[g, kt], p_s[g, slot], preferred_element_type=F32)

    def flash_init(g):
        m_s[g] = jnp.full(m_s.shape[1:], NEG_INF, F32)
        acc_s[g] = jnp.zeros(acc_s.shape[1:], F32)

    def flash_result(g):
        acc = acc_s[g]
        return acc[0:HEAD_DIM] * (1.0 / acc[HEAD_DIM:HEAD_DIM + 1])

    gates = _sigmoid(blog_ref[0].T)
    zero_t = jnp.zeros((tile, LANES), F32)
    blk = _row_iota((nblk, tile))
    tpos = q0 + _lane_iota((nblk, tile))
    groups = range(NSA_KV)
    causal = qpos >= q0 + krow

    def cmp_scores(g):
        qn = stack_heads(bqn_ref, g, zero_t)
        return lax.dot_general(kcd_ref[0, g], qn, nt, preferred_element_type=F32)

    def cmp_probs(s):
        tq = q0 + (_lane_iota((ncmp, cols)) % tile)
        cvis = (_row_iota((ncmp, cols)) * CMP_STRIDE + (CMP_LEN - 1)) <= tq
        s = jnp.where(cvis, s, NEG_INF)
        m = jnp.max(s, axis=0, keepdims=True)
        e = jnp.where(cvis, jnp.exp2(s - m), 0.0)
        den = jnp.sum(e, axis=0, keepdims=True)
        return e * jnp.where(den > 0.0, 1.0 / den, 0.0)

    def cmp_products(g, p):
        o_cmp = jnp.dot(vct_ref[0, g, 0:HEAD_DIM, :], p.astype(BF16), preferred_element_type=F32)
        psum = p[:, 0:tile] + p[:, tile:2 * tile] + p[:, 2 * tile:3 * tile] + p[:, 3 * tile:4 * tile]
        p_hi = psum.astype(BF16)
        p_lo = (psum - p_hi.astype(F32)).astype(BF16)
        ovl = ovl_ref[...]
        imp = (jnp.dot(ovl, p_hi, preferred_element_type=F32)
               + jnp.dot(ovl, p_lo, preferred_element_type=F32))
        return o_cmp, imp

    def select_queries(g, imp):
        cur = tpos // SEL_LEN
        forced = (blk == 0) | (blk == cur) | (blk == cur - 1)
        imp = jnp.where(forced, FORCE_SCORE, imp)
        imp = jnp.where(blk * SEL_LEN <= tpos, imp, NEG_INF)
        rank = jnp.zeros((nblk, tile), F32)
        for k in range(nblk):
            vk = imp[k:k + 1, :]
            tie = jnp.where(blk > k, 1.0, 0.0)
            rank = rank + jnp.where(vk > imp, 1.0, jnp.where(vk == imp, tie, 0.0))
        notsel = jnp.where(rank >= SEL_TOPK, 1.0, 0.0)
        extra = jnp.concatenate([jnp.zeros((HEAD_DIM, tile), F32), notsel,
                                 jnp.zeros((LANES - HEAD_DIM - nblk, tile), F32)], axis=0).T.astype(BF16)
        return q_win[g] + jnp.concatenate([extra] * NSA_REP, axis=0)

    far, near = qi - 2, qi - 1
    keep_far = (qpos - (far * tile + krow)) <= jnp.where(far >= 0, WIN_LEN - 1, -1)
    keep_near = krow >= jnp.where(near >= 0, 0, tile)
    q_win = [stack_heads(bqr_ref, g, zero_t) for g in groups]
    for g in groups:
        flash_init(g)
        scores(kwin_s, g, jnp.maximum(far, 0), q_win[g], 0)
    for g in groups:
        scores(kwin_s, g, jnp.maximum(near, 0), q_win[g], 1)
    cmp_s = [cmp_scores(g) for g in groups]
    for g in groups:
        softmax(g, 0, keep_far)
    cmp_p = [cmp_probs(s) for s in cmp_s]
    for g in groups:
        scores(kwin_s, g, qi, q_win[g], 0)
    for g in groups:
        pv(vwin_s, g, jnp.maximum(far, 0), 0)
    o_cmps, imps = zip(*[cmp_products(g, cmp_p[g]) for g in groups])
    for g in groups:
        softmax(g, 1, keep_near)
    for g in groups:
        pv(vwin_s, g, jnp.maximum(near, 0), 1)
    q_sel = [select_queries(g, imps[g]) for g in groups]
    for g in groups:
        softmax(g, 0, causal)
    for g in groups:
        pv(vwin_s, g, qi, 0)
    o_wins = [flash_result(g) for g in groups]

    odd = qi & 1
    for g in groups:
        flash_init(g)

    @pl.when(odd == 1)
    def _():
        for g in groups:
            scores(ksel_s, g, 0, q_sel[g], 0)
        for g in groups:
            softmax(g, 0, None)
        for g in groups:
            pv(vsel_s, g, 0, 0)

    for g in groups:
        p_s[g, 1] = jnp.zeros(p_s.shape[2:], BF16)
        alpha_s[g, 1] = jnp.ones(alpha_s.shape[2:], F32)
        scores(ksel_s, g, odd, q_sel[g], 0)

    def stage(j, slot):
        for g in groups:
            scores(ksel_s, g, j + 1, q_sel[g], 1 - slot)
        for g in groups:
            pv(vsel_s, g, jnp.maximum(j - 1, 0), 1 - slot)
        for g in groups:
            softmax(g, slot, None)

    def sel_body(jj, carry):
        j = odd + 2 * jj
        stage(j, 0)
        stage(j + 1, 1)
        return carry

    lax.fori_loop(0, (qi - odd) // 2, sel_body, 0)
    for g in groups:
        pv(vsel_s, g, jnp.maximum(qi - 1, 0), 1)
    for g in groups:
        softmax(g, 0, causal)
    for g in groups:
        pv(vsel_s, g, qi, 0)
    o_sels = [flash_result(g) for g in groups]

    heads = []
    for g in groups:
        for r in range(NSA_REP):
            c = (g * NSA_REP + r) * 3
            sl = slice(r * tile, (r + 1) * tile)
            heads.append(gates[c:c + 1] * o_cmps[g][:, sl] + gates[c + 1:c + 2] * o_sels[g][:, sl]
                         + gates[c + 2:c + 3] * o_wins[g][:, sl])
    y_ref[0] = jnp.concatenate(heads, axis=0).T.astype(y_ref.dtype)


def _overlap_matrix(seq):
    starts = np.arange(seq // CMP_STRIDE) * CMP_STRIDE
    bstart = np.arange(seq // SEL_LEN) * SEL_LEN
    ovl = (starts[None, :] < bstart[:, None] + SEL_LEN) & (starts[None, :] + CMP_LEN > bstart[:, None])
    return jnp.asarray(ovl, dtype=BF16)


def _nsa_attention(bqr, bqn, blog, kcd, vct, ksl, vsl, kwn, vwn):
    batch, seq, _ = bqr.shape
    tile = NSA_TILE
    assert 2 * tile == WIN_LEN
    cols = NSA_REP * tile
    ncmp = seq // CMP_STRIDE
    nblk = seq // SEL_LEN
    qspec = lambda w: pl.BlockSpec((1, tile, w), lambda b, i: (b, i, 0))
    kvspec = pl.BlockSpec((1, seq, LANES), lambda b, i: (b, 0, 0))
    cspec = pl.BlockSpec((1, NSA_KV, ncmp, LANES), lambda b, i: (b, 0, 0, 0))
    k_scratch = pltpu.VMEM((NSA_KV, seq, LANES), BF16)
    v_scratch = pltpu.VMEM((NSA_KV, seq // tile, V_ROWS, tile), BF16)
    return pl.pallas_call(
        functools.partial(_nsa_kernel, tile=tile, seq=seq),
        out_shape=jax.ShapeDtypeStruct((batch, seq, NSA_Q_WIDTH), BF16),
        grid=(batch, seq // tile),
        in_specs=[qspec(NSA_Q_WIDTH), qspec(NSA_Q_WIDTH), qspec(LANES), cspec, cspec,
                  kvspec, kvspec, kvspec, kvspec,
                  pl.BlockSpec((nblk, ncmp), lambda b, i: (0, 0))],
        out_specs=qspec(NSA_Q_WIDTH),
        scratch_shapes=[k_scratch, v_scratch, k_scratch, v_scratch,
                        pltpu.VMEM((NSA_KV, 1, cols), F32), pltpu.VMEM((NSA_KV, V_ROWS, cols), F32),
                        pltpu.VMEM((NSA_KV, 2, 1, cols), F32), pltpu.VMEM((NSA_KV, 2, tile, cols), F32),
                        pltpu.VMEM((NSA_KV, 2, tile, cols), BF16)],
        compiler_params=_params(("arbitrary", "arbitrary")),
        name="nsa_attention",
    )(bqr, bqn, blog, kcd, vct, ksl, vsl, kwn, vwn, _overlap_matrix(seq))


def _merge_kernel(o0_ref, o1_ref, o2_ref, l0_ref, l1_ref, l2_ref, yb_ref, ga_ref, gb_ref, x_ref,
                  gate_ref, wa_ref, wb_ref, wo_ref, out_ref, il_s):
    tm = x_ref.shape[1]

    def natural(ref, dil, slot):
        if dil == 1:
            return ref[0].astype(F32)
        nblk = A_GROUP_WIDTH // LANES
        for r in range(dil):
            for j in range(nblk):
                c = r * A_GROUP_WIDTH + j * LANES
                il_s[slot, j, pl.ds(r, tm // dil, stride=dil), :] = ref[0, :, c:c + LANES].astype(F32)
        return jnp.concatenate([il_s[slot, j] for j in range(nblk)], axis=1)

    dils = [dil for _, dil in DIL_GROUPS]
    o0, o1, o2 = (natural(r, dil, n) for n, (r, dil) in enumerate(zip((o0_ref, o1_ref, o2_ref), dils)))
    l0, l1, l2 = (natural(r, dil, 3 + n) for n, (r, dil) in enumerate(zip((l0_ref, l1_ref, l2_ref), dils)))
    m = jnp.maximum(jnp.maximum(l0, l1), l2)
    e0, e1, e2 = jnp.exp(l0 - m), jnp.exp(l1 - m), jnp.exp(l2 - m)
    inv = 1.0 / (e0 + e1 + e2)
    y_a = (e0 * inv) * o0 + (e1 * inv) * o1 + (e2 * inv) * o2
    pa = jnp.dot(y_a.astype(BF16), wa_ref[...], preferred_element_type=F32)
    pb = jnp.dot(yb_ref[0], wb_ref[...], preferred_element_type=F32)
    merged = _sigmoid(ga_ref[0].astype(F32)) * pa + _sigmoid(gb_ref[0].astype(F32)) * pb
    out = jnp.dot(merged.astype(BF16), wo_ref[...], preferred_element_type=F32)
    out_ref[0] = x_ref[0] + gate_ref[...] * out


def _merge(o_lse, y_b, ga, gb, x, mod_l, w_a, w_b, w_o):
    batch, seq, d = x.shape
    tm = ROW_TILE
    row = lambda w, dil=1: pl.BlockSpec((1, tm // dil, dil * w), lambda b, i: (b, i, 0))
    full = lambda a: _resident(a.shape)
    (o0, l0), (o1, l1), (o2, l2) = o_lse
    grp = [row(A_GROUP_WIDTH, dil) for _, dil in DIL_GROUPS]
    return pl.pallas_call(
        _merge_kernel,
        out_shape=jax.ShapeDtypeStruct(x.shape, F32),
        grid=(batch, seq // tm),
        in_specs=grp + grp + [row(NSA_Q_WIDTH), row(d), row(d), row(d),
                              pl.BlockSpec((None, None, 1, d), lambda b, i: (b, 2, 0, 0)),
                              full(w_a), full(w_b), full(w_o)],
        out_specs=row(d),
        scratch_shapes=[pltpu.VMEM((6, A_GROUP_WIDTH // LANES, tm, LANES), F32)],
        compiler_params=_params(("arbitrary", "arbitrary")),
        name="branch_merge",
    )(o0, o1, o2, l0, l1, l2, y_b, ga, gb, x, mod_l, w_a, w_b, w_o)


def _ffn_kernel(x_ref, halo_ref, shift_ref, scale_ref, gate_ref, g_ref, wup_ref, cw_ref, cb_ref,
                wdn_ref, fin_ref, out_ref, he_s, *, final):
    i = pl.program_id(1)
    tm = x_ref.shape[1]
    hh = _norm_mod(halo_ref[0], g_ref[...], scale_ref[...], shift_ref[...]) * jnp.where(i > 0, 1.0, 0.0)
    he_s[0:tm] = _norm_mod(x_ref[0], g_ref[...], scale_ref[...], shift_ref[...]).astype(BF16)
    he_s[tm:] = hh.astype(BF16)

    def up(col):
        return jnp.dot(he_s[...], wup_ref[:, col:col + FFN_CHUNK], preferred_element_type=F32)

    def conv(u, col):
        w = cw_ref[:, col:col + FFN_CHUNK]
        return (w[0:1] * pltpu.roll(u, 2, 0)[:tm] + w[1:2] * pltpu.roll(u, 1, 0)[:tm]
                + (w[2:3] * u[:tm] + cb_ref[:, col:col + FFN_CHUNK]))

    chunks = list(range(0, D_FF, FFN_CHUNK))
    pending = [(up(c), up(D_FF + c)) for c in chunks[:FFN_LOOKAHEAD]]
    for n, c in enumerate(chunks):
        ug, uv = pending.pop(0)
        if n + FFN_LOOKAHEAD < len(chunks):
            ahead = chunks[n + FFN_LOOKAHEAD]
            pending.append((up(ahead), up(D_FF + ahead)))
        act = (_silu(conv(ug, c)) * conv(uv, D_FF + c)).astype(BF16)
        part = jnp.dot(act, wdn_ref[c:c + FFN_CHUNK, :], preferred_element_type=F32)
        if n == 0:
            out_ref[0] = part
        else:
            out_ref[0] += part
    y = x_ref[0] + gate_ref[...] * out_ref[0]
    if final:
        ms = jnp.mean(y * y, axis=-1, keepdims=True)
        y = y * lax.rsqrt(ms + RMS_EPS) * fin_ref[...]
    out_ref[0] = y


def _ffn(x, mod_l, norm_g, w_up, conv_w, conv_b, w_down, final_g, final):
    batch, seq, d = x.shape
    tm = ROW_TILE
    row = pl.BlockSpec((1, tm, d), lambda b, i: (b, i, 0))
    halo = pl.BlockSpec((1, HALO, d), lambda b, i: (b, jnp.maximum(i * (tm // HALO) - 1, 0), 0))
    vec = lambda k: pl.BlockSpec((None, None, 1, d), lambda b, i, k=k: (b, k, 0, 0))
    full = lambda a: _resident(a.shape)
    norm_g = norm_g.reshape(1, d)
    conv_b = conv_b.reshape(1, -1)
    final_g = final_g.reshape(1, d)
    return pl.pallas_call(
        functools.partial(_ffn_kernel, final=final),
        out_shape=jax.ShapeDtypeStruct(x.shape, F32),
        grid=(batch, seq // tm),
        in_specs=[row, halo, vec(3), vec(4), vec(5), full(norm_g), full(w_up), full(conv_w),
                  full(conv_b), full(w_down), full(final_g)],
        out_specs=row,
        scratch_shapes=[pltpu.VMEM((tm + HALO, d), BF16)],
        compiler_params=_params(("arbitrary", "arbitrary")),
        name="conv_ffn",
    )(x, x, mod_l, mod_l, mod_l, norm_g, w_up, conv_w, conv_b, w_down, final_g)


def _rope_tables(seq):
    inv = 1.0 / (ROPE_THETA ** (jnp.arange(0, HEAD_DIM, 2, dtype=F32) / HEAD_DIM))
    ang = jnp.arange(seq, dtype=F32)[:, None] * inv[None, :]
    cos, sin = jnp.cos(ang), jnp.sin(ang)
    cos_t = jnp.tile(cos, (1, LANES // (HEAD_DIM // 2)))
    sin_t = jnp.tile(jnp.concatenate([-sin, sin], axis=1), (1, LANES // HEAD_DIM))
    return cos_t, sin_t


def kernel(x, c, norm1_g, norm2_g, final_g, w_mod, b_mod, w_in, cmp_pe_k, cmp_pe_v, cmp_w1_k, cmp_w2_k, cmp_w1_v, cmp_w2_v, w_br_a, w_br_b, w_out, w_up, conv_w, conv_b, w_down):
    batch, seq, d = x.shape
    depth = w_in.shape[0]
    cos_t, sin_t = _rope_tables(seq)
    mod = _modulation(c, w_mod, b_mod).reshape(depth, batch, N_MOD, 1, d)
    for layer in range(depth):
        mod_l = mod[layer]
        (qkv0, qkv1, qkv2, bqr, bqn, kc_raw, vc_raw, ksl, vsl, kwn, vwn, blog, ga, gb) = _input_projection(
            x, mod_l, norm1_g[layer], cos_t, sin_t, _pack_w_in(w_in[layer]))
        o_lse = [_banded_attention(qkv, dil) for qkv, (_, dil) in zip((qkv0, qkv1, qkv2), DIL_GROUPS)]

        pk = _pack_compress_weights(cmp_pe_k[layer], cmp_w1_k[layer], cmp_w2_k[layer])
        pv = _pack_compress_weights(cmp_pe_v[layer], cmp_w1_v[layer], cmp_w2_v[layer])
        pep, w1p, w2p = (jnp.stack([a, b]) for a, b in zip(pk, pv))
        kcd, vcd = _compress(kc_raw, vc_raw, pep, w1p, w2p)
        y_b = _nsa_attention(bqr, bqn, blog, kcd, vcd, ksl, vsl, kwn, vwn)

        x = _merge(o_lse, y_b, ga, gb, x, mod_l, w_br_a[layer].astype(BF16),
                   w_br_b[layer].astype(BF16), w_out[layer].astype(BF16))
        x = _ffn(x, mod_l, norm2_g[layer], w_up[layer].astype(BF16), conv_w[layer], conv_b[layer],
                 w_down[layer].astype(BF16), final_g, final=(layer == depth - 1))
    return x
```

```python
import functools

import jax
import jax.numpy as jnp
import numpy as np
from jax import lax
from jax.experimental import pallas as pl
from jax.experimental.pallas import tpu as pltpu

F32 = jnp.float32
BF16 = jnp.bfloat16

HEAD_DIM = 64
ROPE_THETA = 10000.0
RMS_EPS = 1e-6
NEG_INF = -1e30
FORCE_SCORE = 1e9
MASK_BIG = float(2.0 ** 100)
LOG2E = 1.4426950408889634
LN2 = 0.6931471805599453

DIL_GROUPS = ((128, 1), (512, 4), (2048, 16))
A_HEADS = 4
A_GROUP_WIDTH = A_HEADS * HEAD_DIM
BAND_BACK = 128
NSA_HEADS = 8
NSA_KV = 2
NSA_REP = NSA_HEADS // NSA_KV
NSA_Q_WIDTH = NSA_HEADS * HEAD_DIM
CMP_LEN = 32
CMP_STRIDE = 16
CMP_HID = 256
SEL_LEN = 64
SEL_TOPK = 16
WIN_LEN = 512
D_FF = 2816
CONV_W = 3
N_MOD = 6

LANES = 128
VMEM_LIMIT = 56 * 1024 * 1024

ROW_TILE = 512
NSA_TILE = 256
BANDED_UNITS = 8
FFN_CHUNK = 256
FFN_LOOKAHEAD = 2
FFN_DOWN_SPLITS = 1
HALO = 8
V_ROWS = 80


def _sigmoid(x):
    return 0.5 * jnp.tanh(0.5 * x) + 0.5


def _silu(x):
    h = 0.5 * x
    return h + h * jnp.tanh(h)


def _params(semantics):
    return pltpu.CompilerParams(dimension_semantics=semantics, vmem_limit_bytes=VMEM_LIMIT)


def _resident(shape):
    nd = len(shape)
    return pl.BlockSpec(tuple(shape), lambda *_: (0,) * nd, pipeline_mode=pl.Buffered(1))


def _lane_iota(shape):
    return lax.broadcasted_iota(jnp.int32, shape, len(shape) - 1)


def _row_iota(shape):
    return lax.broadcasted_iota(jnp.int32, shape, len(shape) - 2)


def _mod_kernel(c_ref, w_ref, b_ref, o_ref):
    sc = _silu(c_ref[...])
    o_ref[0] = jnp.dot(sc, w_ref[0], preferred_element_type=F32) + b_ref[0]


def _modulation(c, w_mod, b_mod):
    depth, d, n = w_mod.shape
    batch = c.shape[0]
    tn = 2048
    return pl.pallas_call(
        _mod_kernel,
        out_shape=jax.ShapeDtypeStruct((depth, batch, n), F32),
        grid=(depth, n // tn),
        in_specs=[
            pl.BlockSpec((batch, d), lambda l, j: (0, 0)),
            pl.BlockSpec((1, d, tn), lambda l, j: (l, 0, j)),
            pl.BlockSpec((1, 1, tn), lambda l, j: (l, 0, j)),
        ],
        out_specs=pl.BlockSpec((1, batch, tn), lambda l, j: (l, 0, j)),
        compiler_params=_params(("arbitrary", "arbitrary")),
        name="modulation",
    )(c, w_mod, b_mod.reshape(depth, 1, n))


def _norm_mod(x, g, scale, shift):
    ms = jnp.mean(x * x, axis=-1, keepdims=True)
    y = x * lax.rsqrt(ms + RMS_EPS) * g
    return y * (1.0 + scale) + shift


def _rope_chunk(xc, cos, sin_signed, first_half):
    partner = jnp.where(first_half, pltpu.roll(xc, LANES - HEAD_DIM // 2, 1),
                        pltpu.roll(xc, HEAD_DIM // 2, 1))
    return xc * cos + partner * sin_signed


_QKV_A = 3 * A_GROUP_WIDTH
_OFF_BQ = 3 * _QKV_A
_OFF_KV = _OFF_BQ + NSA_Q_WIDTH
_OFF_BLOG = _OFF_KV + 6 * LANES
_OFF_GA = _OFF_BLOG + LANES
N_IN_PACKED = _OFF_GA + 2 * 1024


def _inproj_kernel(x_ref, shift_ref, scale_ref, g_ref, cos_ref, sin_ref, w_ref,
                   qkv0_ref, qkv1_ref, qkv2_ref, bqr_ref, bqn_ref, kc_ref, vc_ref,
                   ksl_ref, vsl_ref, kwn_ref, vwn_ref, blog_ref, ga_ref, gb_ref, il_s):
    h = _norm_mod(x_ref[0], g_ref[...], scale_ref[...], shift_ref[...]).astype(BF16)
    cos = cos_ref[...]
    sin = sin_ref[...]
    rows = cos.shape[0]
    first_half = (_lane_iota((rows, LANES)) % HEAD_DIM) < (HEAD_DIM // 2)
    scale = HEAD_DIM ** -0.5 * LOG2E

    def proj(off, width):
        return jnp.dot(h, w_ref[:, off:off + width], preferred_element_type=F32)

    def rope(y, mult):
        parts = [_rope_chunk(y[:, c:c + LANES], cos, sin, first_half) * mult
                 for c in range(0, y.shape[1], LANES)]
        return parts[0] if len(parts) == 1 else jnp.concatenate(parts, axis=1)

    def store_strided(out_ref, y, dil, col):
        nblk = y.shape[1] // LANES
        width = out_ref.shape[2] // dil
        for j in range(nblk):
            il_s[j] = y[:, j * LANES:(j + 1) * LANES]
        for r in range(dil):
            for j in range(nblk):
                c = r * width + col + j * LANES
                out_ref[0, :, c:c + LANES] = il_s[j, pl.ds(r, rows // dil, stride=dil), :].astype(out_ref.dtype)

    for (_, dil), out_ref, base in zip(DIL_GROUPS, (qkv0_ref, qkv1_ref, qkv2_ref), (0, _QKV_A, 2 * _QKV_A)):
        pieces = (rope(proj(base, 256), scale), rope(proj(base + 256, 256), 1.0), proj(base + 512, 256))
        for n, y in enumerate(pieces):
            if dil == 1:
                out_ref[0, :, n * 256:(n + 1) * 256] = y.astype(out_ref.dtype)
            else:
                store_strided(out_ref, y, dil, n * 256)

    bq = proj(_OFF_BQ, NSA_Q_WIDTH)
    bqn_ref[0] = (bq * scale).astype(bqn_ref.dtype)
    bqr_ref[0] = rope(bq, scale).astype(bqr_ref.dtype)

    kv = proj(_OFF_KV, 6 * LANES)
    store_strided(kc_ref, kv[:, 0:128], CMP_STRIDE, 0)
    store_strided(vc_ref, kv[:, 128:256], CMP_STRIDE, 0)
    ksl_ref[0] = rope(kv[:, 256:384], 1.0).astype(ksl_ref.dtype)
    vsl_ref[0] = kv[:, 384:512].astype(vsl_ref.dtype)
    kwn_ref[0] = rope(kv[:, 512:640], 1.0).astype(kwn_ref.dtype)
    vwn_ref[0] = kv[:, 640:768].astype(vwn_ref.dtype)

    blog_ref[0] = proj(_OFF_BLOG, LANES)
    ga_ref[0] = proj(_OFF_GA, 1024).astype(ga_ref.dtype)
    gb_ref[0] = proj(_OFF_GA + 1024, 1024).astype(gb_ref.dtype)


def _pack_w_in(w):
    aq, ak, av = w[:, 0:768], w[:, 768:1536], w[:, 1536:2304]
    cols = []
    for g in range(3):
        sl = slice(g * 256, (g + 1) * 256)
        cols += [aq[:, sl], ak[:, sl], av[:, sl]]
    cols.append(w[:, 2304:3584])
    cols.append(jnp.pad(w[:, 3584:3608], ((0, 0), (0, LANES - 24))))
    cols.append(w[:, 3608:5656])
    return jnp.concatenate(cols, axis=1).astype(BF16)


def _input_projection(x, mod_l, norm_g, cos_t, sin_t, w_packed):
    batch, seq, d = x.shape
    tm = ROW_TILE
    row = lambda w, dil=1: pl.BlockSpec((1, tm // dil, dil * w), lambda b, i: (b, i, 0))
    vec = lambda k: pl.BlockSpec((None, None, 1, d), lambda b, i, k=k: (b, k, 0, 0))
    shp = lambda w, dt, dil=1: jax.ShapeDtypeStruct((batch, seq // dil, dil * w), dt)
    dils = [dil for _, dil in DIL_GROUPS]
    return pl.pallas_call(
        _inproj_kernel,
        out_shape=[shp(_QKV_A, BF16, dil) for dil in dils] + [shp(512, BF16)] * 2
        + [shp(128, BF16, CMP_STRIDE)] * 2 + [shp(128, F32)] * 4 + [shp(128, F32)] + [shp(1024, BF16)] * 2,
        grid=(batch, seq // tm),
        in_specs=[
            row(d), vec(0), vec(1),
            pl.BlockSpec((1, d), lambda b, i: (0, 0)),
            pl.BlockSpec((tm, LANES), lambda b, i: (i, 0)),
            pl.BlockSpec((tm, LANES), lambda b, i: (i, 0)),
            _resident((d, N_IN_PACKED)),
        ],
        out_specs=[row(_QKV_A, dil) for dil in dils] + [row(512)] * 2 + [row(128, CMP_STRIDE)] * 2
        + [row(128)] * 5 + [row(1024)] * 2,
        scratch_shapes=[pltpu.VMEM((A_GROUP_WIDTH // LANES, tm, LANES), F32)],
        compiler_params=_params(("arbitrary", "arbitrary")),
        name="input_projection",
    )(x, mod_l, mod_l, norm_g.reshape(1, d), cos_t, sin_t, w_packed)


def _banded_kernel(qkv_ref, o_ref, lse_ref, *, length, nres):
    blk = 128
    nq = max(1, min(BANDED_UNITS // (2 * nres), length // blk))
    nkeys = 2 * blk if length > blk else blk
    lane = _lane_iota((blk, LANES))
    lo = lane < HEAD_DIM
    qrow = _row_iota((2 * blk, nkeys)) % blk
    kcol = _lane_iota((2 * blk, nkeys))

    def rows_of(i, qb):
        r0 = pl.multiple_of((i * nq + qb) * blk, blk)
        ks = pl.multiple_of(jnp.maximum(r0 - blk, 0), blk) if length > blk else 0
        return r0, ks

    def score(i, qb, res, pair):
        base = res * _QKV_A + pair * LANES
        r0, ks = rows_of(i, qb)
        qp = qkv_ref[0, pl.ds(r0, blk), base:base + LANES]
        k2 = qkv_ref[0, pl.ds(ks, nkeys), base + 256:base + 256 + LANES]
        zero = jnp.zeros_like(qp)
        qs = jnp.concatenate([jnp.where(lo, qp, zero), jnp.where(lo, zero, qp)], axis=0)
        s = lax.dot_general(qs, k2, (((1,), (1,)), ((), ())), preferred_element_type=F32)
        dist = (r0 + qrow) - (ks + kcol)
        return jnp.where((dist >= 0) & (dist <= BAND_BACK), s, NEG_INF)

    def soft(s):
        m = jnp.max(s, axis=-1, keepdims=True)
        e = jnp.exp2(s - m)
        den = jnp.sum(e, axis=-1, keepdims=True)
        return e.astype(BF16), 1.0 / den, m * LN2 + jnp.log(den)

    def emit(i, qb, res, pair, e, inv, lse):
        base = res * _QKV_A + pair * LANES
        r0, ks = rows_of(i, qb)
        v2 = qkv_ref[0, pl.ds(ks, nkeys), base + 512:base + 512 + LANES]
        o = jnp.dot(e, v2, preferred_element_type=F32) * inv
        ob = res * A_GROUP_WIDTH + pair * LANES
        o_ref[0, pl.ds(r0, blk), ob:ob + LANES] = jnp.where(lo, o[:blk], o[blk:]).astype(o_ref.dtype)
        lse_ref[0, pl.ds(r0, blk), ob:ob + LANES] = jnp.where(
            lo, jnp.broadcast_to(lse[:blk], (blk, LANES)), jnp.broadcast_to(lse[blk:], (blk, LANES)))

    units = [(qb, res, pair) for qb in range(nq) for res in range(nres) for pair in range(2)]

    def body(i, carry):
        ss = [score(i, *u) for u in units]
        ps = [soft(s) for s in ss]
        for u, p in zip(units, ps):
            emit(i, *u, *p)
        return carry

    lax.fori_loop(0, length // (blk * nq), body, 0)


def _banded_attention(view, dil):
    batch, length, _ = view.shape
    nres = min(dil, 4)
    return pl.pallas_call(
        functools.partial(_banded_kernel, length=length, nres=nres),
        out_shape=[jax.ShapeDtypeStruct((batch, length, dil * A_GROUP_WIDTH), BF16),
                   jax.ShapeDtypeStruct((batch, length, dil * A_GROUP_WIDTH), F32)],
        grid=(batch, dil // nres),
        in_specs=[pl.BlockSpec((1, length, nres * _QKV_A), lambda b, r: (b, 0, r))],
        out_specs=[pl.BlockSpec((1, length, nres * A_GROUP_WIDTH), lambda b, r: (b, 0, r))] * 2,
        compiler_params=_params(("arbitrary", "arbitrary")),
        name=f"banded_attention_d{dil}",
    )(view)


def _compress_kernel(k_ref, v_ref, pe_ref, w1_ref, w2_ref, kc_ref, vc_ref):
    nchunk = k_ref.shape[1]
    for t, (src, dst) in enumerate(((k_ref, kc_ref), (v_ref, vc_ref))):
        c = src[0].astype(F32)
        top = jnp.dot((c + pe_ref[t, 0]).astype(BF16), w1_ref[t, 0], preferred_element_type=F32)
        bot = jnp.dot((c + pe_ref[t, 1]).astype(BF16), w1_ref[t, 1], preferred_element_type=F32)
        pre = top + pltpu.roll(bot, nchunk - 1, 0)
        for g in range(NSA_KV):
            hid = _silu(pre[:, g * CMP_HID:(g + 1) * CMP_HID]).astype(BF16)
            res = jnp.dot(hid, w2_ref[t], preferred_element_type=F32)
            dst[0, g] = (res if t == 0 else res.T).astype(dst.dtype)


def _pack_compress_weights(pe, w1, w2):
    half = CMP_LEN // 2
    w1r = w1.reshape(2, half, HEAD_DIM, CMP_HID)
    zero = jnp.zeros_like(w1r)
    w1p = jnp.stack([jnp.concatenate([w1r, zero], axis=-1), jnp.concatenate([zero, w1r], axis=-1)], axis=2)
    w1p = w1p.reshape(2, half * LANES, NSA_KV * CMP_HID)
    pep = jnp.tile(pe.reshape(2, half, 1, HEAD_DIM), (1, 1, NSA_KV, 1)).reshape(2, 1, half * LANES)
    w2p = jnp.concatenate([w2, w2], axis=1)
    return pep, w1p.astype(BF16), w2p.astype(BF16)


def _compress(kc_raw, vc_raw, pep, w1p, w2p):
    batch, nchunk, width = kc_raw.shape
    view = lambda a: a
    full = lambda a: _resident(a.shape)
    out = jax.ShapeDtypeStruct((batch, NSA_KV, nchunk, LANES), BF16)
    return pl.pallas_call(
        _compress_kernel,
        out_shape=[out, out],
        grid=(batch,),
        in_specs=[pl.BlockSpec((1, nchunk, width), lambda b: (b, 0, 0))] * 2
        + [full(pep), full(w1p), full(w2p)],
        out_specs=[pl.BlockSpec((1, NSA_KV, nchunk, LANES), lambda b: (b, 0, 0, 0))] * 2,
        compiler_params=_params(("arbitrary",)),
        name="nsa_compress",
    )(view(kc_raw), view(vc_raw), pep, w1p, w2p)


def _nsa_kernel(bqr_ref, bqn_ref, blog_ref, kcd_ref, vct_ref, ksl_ref, vsl_ref, kwn_ref, vwn_ref,
                ovl_ref, y_ref, ksel_s, vsel_s, kwin_s, vwin_s, m_s, acc_s, alpha_s, s_s, p_s, *, tile, seq):
    qi = pl.program_id(1)
    q0 = qi * tile
    cols = NSA_REP * tile
    nblk = seq // SEL_LEN
    ncmp = kcd_ref.shape[2]
    ntile = seq // tile
    nt = (((1,), (1,)), ((), ()))

    @pl.when(qi == 0)
    def _build_kv():
        lane = _lane_iota((seq, LANES))
        lo = lane < HEAD_DIM
        blk_of_row = _row_iota((seq, LANES)) // SEL_LEN
        selmask = jnp.where(blk_of_row == lane - HEAD_DIM, -MASK_BIG, 0.0)
        for src, dst, fill in ((ksl_ref, ksel_s, selmask), (kwn_ref, kwin_s, 0.0)):
            a = src[0]
            dst[0] = jnp.where(lo, a, fill).astype(dst.dtype)
            dst[1] = jnp.where(lo, pltpu.roll(a, HEAD_DIM, 1), fill).astype(dst.dtype)
        ones_rows = jnp.where(_row_iota((V_ROWS - HEAD_DIM, seq)) == 0, 1.0, 0.0)
        for src, dst in ((vsl_ref, vsel_s), (vwn_ref, vwin_s)):
            vt = src[0].T
            for g in range(NSA_KV):
                full = jnp.concatenate([vt[g * HEAD_DIM:(g + 1) * HEAD_DIM], ones_rows], axis=0).astype(dst.dtype)
                for kt in range(ntile):
                    dst[g, kt] = full[:, kt * tile:(kt + 1) * tile]

    lo_t = _lane_iota((tile, LANES)) < HEAD_DIM

    def stack_heads(q_ref, g, extra):
        out = []
        for pair in range(2):
            c = g * 256 + pair * LANES
            qp = q_ref[0, :, c:c + LANES].astype(F32)
            out.append(jnp.where(lo_t, qp, extra))
            out.append(jnp.where(lo_t, pltpu.roll(qp, HEAD_DIM, 1), extra))
        return jnp.concatenate(out, axis=0).astype(BF16)

    qpos = q0 + (_lane_iota((tile, cols)) % tile)
    krow = _row_iota((tile, cols))

    def scores(k_s, g, kt, qs, slot):
        k0 = pl.multiple_of(kt * tile, tile)
        s_s[g, slot] = lax.dot_general(k_s[g, pl.ds(k0, tile), :], qs, nt, preferred_element_type=F32)

    def softmax(g, slot, keep):
        s = s_s[g, slot]
        if keep is not None:
            s = jnp.where(keep, s, NEG_INF)
        m_old = m_s[g]
        m_new = jnp.maximum(m_old, jnp.max(s, axis=0, keepdims=True))
        alpha_s[g, slot] = jnp.exp2(m_old - m_new)
        p_s[g, slot] = jnp.exp2(s - m_new).astype(BF16)
        m_s[g] = m_new

    def pv(v_s, g, kt, slot):
        acc_s[g] = alpha_s[g, slot] * acc_s[g] + jnp.dot(v_s[g, kt], p_s[g, slot], preferred_element_type=F32)

    def flash_init(g):
        m_s[g] = jnp.full(m_s.shape[1:], NEG_INF, F32)
        acc_s[g] = jnp.zeros(acc_s.shape[1:], F32)

    def flash_result(g):
        acc = acc_s[g]
        return acc[0:HEAD_DIM] * (1.0 / acc[HEAD_DIM:HEAD_DIM + 1])

    gates = _sigmoid(blog_ref[0].T)
    zero_t = jnp.zeros((tile, LANES), F32)
    blk = _row_iota((nblk, tile))
    tpos = q0 + _lane_iota((nblk, tile))
    groups = range(NSA_KV)
    causal = qpos >= q0 + krow

    def cmp_scores(g):
        qn = stack_heads(bqn_ref, g, zero_t)
        return lax.dot_general(kcd_ref[0, g], qn, nt, preferred_element_type=F32)

    def cmp_probs(s):
        tq = q0 + (_lane_iota((ncmp, cols)) % tile)
        cvis = (_row_iota((ncmp, cols)) * CMP_STRIDE + (CMP_LEN - 1)) <= tq
        s = jnp.where(cvis, s, NEG_INF)
        m = jnp.max(s, axis=0, keepdims=True)
        e = jnp.where(cvis, jnp.exp2(s - m), 0.0)
        den = jnp.sum(e, axis=0, keepdims=True)
        return e * jnp.where(den > 0.0, 1.0 / den, 0.0)

    def cmp_products(g, p):
        o_cmp = jnp.dot(vct_ref[0, g, 0:HEAD_DIM, :], p.astype(BF16), preferred_element_type=F32)
        psum = p[:, 0:tile] + p[:, tile:2 * tile] + p[:, 2 * tile:3 * tile] + p[:, 3 * tile:4 * tile]
        p_hi = psum.astype(BF16)
        p_lo = (psum - p_hi.astype(F32)).astype(BF16)
        ovl = ovl_ref[...]
        imp = (jnp.dot(ovl, p_hi, preferred_element_type=F32)
               + jnp.dot(ovl, p_lo, preferred_element_type=F32))
        return o_cmp, imp

    def select_queries(g, imp):
        cur = tpos // SEL_LEN
        forced = (blk == 0) | (blk == cur) | (blk == cur - 1)
        imp = jnp.where(forced, FORCE_SCORE, imp)
        imp = jnp.where(blk * SEL_LEN <= tpos, imp, NEG_INF)
        rank = jnp.zeros((nblk, tile), F32)
        for k in range(nblk):
            vk = imp[k:k + 1, :]
            tie = jnp.where(blk > k, 1.0, 0.0)
            rank = rank + jnp.where(vk > imp, 1.0, jnp.where(vk == imp, tie, 0.0))
        notsel = jnp.where(rank >= SEL_TOPK, 1.0, 0.0)
        extra = jnp.concatenate([jnp.zeros((HEAD_DIM, tile), F32), notsel,
                                 jnp.zeros((LANES - HEAD_DIM - nblk, tile), F32)], axis=0).T.astype(BF16)
        return q_win[g] + jnp.concatenate([extra] * NSA_REP, axis=0)

    far, near = qi - 2, qi - 1
    keep_far = (qpos - (far * tile + krow)) <= jnp.where(far >= 0, WIN_LEN - 1, -1)
    keep_near = krow >= jnp.where(near >= 0, 0, tile)
    q_win = [stack_heads(bqr_ref, g, zero_t) for g in groups]
    for g in groups:
        flash_init(g)
        scores(kwin_s, g, jnp.maximum(far, 0), q_win[g], 0)
    for g in groups:
        scores(kwin_s, g, jnp.maximum(near, 0), q_win[g], 1)
    cmp_s = [cmp_scores(g) for g in groups]
    for g in groups:
        softmax(g, 0, keep_far)
    cmp_p = [cmp_probs(s) for s in cmp_s]
    for g in groups:
        scores(kwin_s, g, qi, q_win[g], 0)
    for g in groups:
        pv(vwin_s, g, jnp.maximum(far, 0), 0)
    o_cmps, imps = zip(*[cmp_products(g, cmp_p[g]) for g in groups])
    for g in groups:
        softmax(g, 1, keep_near)
    for g in groups:
        pv(vwin_s, g, jnp.maximum(near, 0), 1)
    q_sel = [select_queries(g, imps[g]) for g in groups]
    for g in groups:
        softmax(g, 0, causal)
    for g in groups:
        pv(vwin_s, g, qi, 0)
    o_wins = [flash_result(g) for g in groups]

    odd = qi & 1
    for g in groups:
        flash_init(g)

    @pl.when(odd == 1)
    def _():
        for g in groups:
            scores(ksel_s, g, 0, q_sel[g], 0)
        for g in groups:
            softmax(g, 0, None)
        for g in groups:
            pv(vsel_s, g, 0, 0)

    for g in groups:
        p_s[g, 1] = jnp.zeros(p_s.shape[2:], BF16)
        alpha_s[g, 1] = jnp.ones(alpha_s.shape[2:], F32)
        scores(ksel_s, g, odd, q_sel[g], 0)

    def stage(j, slot):
        for g in groups:
            scores(ksel_s, g, j + 1, q_sel[g], 1 - slot)
        for g in groups:
            pv(vsel_s, g, jnp.maximum(j - 1, 0), 1 - slot)
        for g in groups:
            softmax(g, slot, None)

    def sel_body(jj, carry):
        j = odd + 2 * jj
        stage(j, 0)
        stage(j + 1, 1)
        return carry

    lax.fori_loop(0, (qi - odd) // 2, sel_body, 0)
    for g in groups:
        pv(vsel_s, g, jnp.maximum(qi - 1, 0), 1)
    for g in groups:
        softmax(g, 0, causal)
    for g in groups:
        pv(vsel_s, g, qi, 0)
    o_sels = [flash_result(g) for g in groups]

    heads = []
    for g in groups:
        for r in range(NSA_REP):
            c = (g * NSA_REP + r) * 3
            sl = slice(r * tile, (r + 1) * tile)
            heads.append(gates[c:c + 1] * o_cmps[g][:, sl] + gates[c + 1:c + 2] * o_sels[g][:, sl]
                         + gates[c + 2:c + 3] * o_wins[g][:, sl])
    y_ref[0] = jnp.concatenate(heads, axis=0).T.astype(y_ref.dtype)


def _overlap_matrix(seq):
    starts = np.arange(seq // CMP_STRIDE) * CMP_STRIDE
    bstart = np.arange(seq // SEL_LEN) * SEL_LEN
    ovl = (starts[None, :] < bstart[:, None] + SEL_LEN) & (starts[None, :] + CMP_LEN > bstart[:, None])
    return jnp.asarray(ovl, dtype=BF16)


def _nsa_attention(bqr, bqn, blog, kcd, vct, ksl, vsl, kwn, vwn):
    batch, seq, _ = bqr.shape
    tile = NSA_TILE
    assert 2 * tile == WIN_LEN
    cols = NSA_REP * tile
    ncmp = seq // CMP_STRIDE
    nblk = seq // SEL_LEN
    qspec = lambda w: pl.BlockSpec((1, tile, w), lambda b, i: (b, i, 0))
    kvspec = pl.BlockSpec((1, seq, LANES), lambda b, i: (b, 0, 0))
    cspec = pl.BlockSpec((1, NSA_KV, ncmp, LANES), lambda b, i: (b, 0, 0, 0))
    k_scratch = pltpu.VMEM((NSA_KV, seq, LANES), BF16)
    v_scratch = pltpu.VMEM((NSA_KV, seq // tile, V_ROWS, tile), BF16)
    return pl.pallas_call(
        functools.partial(_nsa_kernel, tile=tile, seq=seq),
        out_shape=jax.ShapeDtypeStruct((batch, seq, NSA_Q_WIDTH), BF16),
        grid=(batch, seq // tile),
        in_specs=[qspec(NSA_Q_WIDTH), qspec(NSA_Q_WIDTH), qspec(LANES), cspec, cspec,
                  kvspec, kvspec, kvspec, kvspec,
                  pl.BlockSpec((nblk, ncmp), lambda b, i: (0, 0))],
        out_specs=qspec(NSA_Q_WIDTH),
        scratch_shapes=[k_scratch, v_scratch, k_scratch, v_scratch,
                        pltpu.VMEM((NSA_KV, 1, cols), F32), pltpu.VMEM((NSA_KV, V_ROWS, cols), F32),
                        pltpu.VMEM((NSA_KV, 2, 1, cols), F32), pltpu.VMEM((NSA_KV, 2, tile, cols), F32),
                        pltpu.VMEM((NSA_KV, 2, tile, cols), BF16)],
        compiler_params=_params(("arbitrary", "arbitrary")),
        name="nsa_attention",
    )(bqr, bqn, blog, kcd, vct, ksl, vsl, kwn, vwn, _overlap_matrix(seq))


def _merge_kernel(o0_ref, o1_ref, o2_ref, l0_ref, l1_ref, l2_ref, yb_ref, ga_ref, gb_ref, x_ref,
                  gate_ref, wa_ref, wb_ref, wo_ref, out_ref, il_s):
    tm = x_ref.shape[1]

    def natural(ref, dil, slot):
        if dil == 1:
            return ref[0].astype(F32)
        nblk = A_GROUP_WIDTH // LANES
        for r in range(dil):
            for j in range(nblk):
                c = r * A_GROUP_WIDTH + j * LANES
                il_s[slot, j, pl.ds(r, tm // dil, stride=dil), :] = ref[0, :, c:c + LANES].astype(F32)
        return jnp.concatenate([il_s[slot, j] for j in range(nblk)], axis=1)

    dils = [dil for _, dil in DIL_GROUPS]
    o0, o1, o2 = (natural(r, dil, n) for n, (r, dil) in enumerate(zip((o0_ref, o1_ref, o2_ref), dils)))
    l0, l1, l2 = (natural(r, dil, 3 + n) for n, (r, dil) in enumerate(zip((l0_ref, l1_ref, l2_ref), dils)))
    m = jnp.maximum(jnp.maximum(l0, l1), l2)
    e0, e1, e2 = jnp.exp(l0 - m), jnp.exp(l1 - m), jnp.exp(l2 - m)
    inv = 1.0 / (e0 + e1 + e2)
    y_a = (e0 * inv) * o0 + (e1 * inv) * o1 + (e2 * inv) * o2
    pa = jnp.dot(y_a.astype(BF16), wa_ref[...], preferred_element_type=F32)
    pb = jnp.dot(yb_ref[0], wb_ref[...], preferred_element_type=F32)
    merged = _sigmoid(ga_ref[0].astype(F32)) * pa + _sigmoid(gb_ref[0].astype(F32)) * pb
    out = jnp.dot(merged.astype(BF16), wo_ref[...], preferred_element_type=F32)
    out_ref[0] = x_ref[0] + gate_ref[...] * out


def _merge(o_lse, y_b, ga, gb, x, mod_l, w_a, w_b, w_o):
    batch, seq, d = x.shape
    tm = ROW_TILE
    row = lambda w, dil=1: pl.BlockSpec((1, tm // dil, dil * w), lambda b, i: (b, i, 0))
    full = lambda a: _resident(a.shape)
    (o0, l0), (o1, l1), (o2, l2) = o_lse
    grp = [row(A_GROUP_WIDTH, dil) for _, dil in DIL_GROUPS]
    return pl.pallas_call(
        _merge_kernel,
        out_shape=jax.ShapeDtypeStruct(x.shape, F32),
        grid=(batch, seq // tm),
        in_specs=grp + grp + [row(NSA_Q_WIDTH), row(d), row(d), row(d),
                              pl.BlockSpec((None, None, 1, d), lambda b, i: (b, 2, 0, 0)),
                              full(w_a), full(w_b), full(w_o)],
        out_specs=row(d),
        scratch_shapes=[pltpu.VMEM((6, A_GROUP_WIDTH // LANES, tm, LANES), F32)],
        compiler_params=_params(("arbitrary", "arbitrary")),
        name="branch_merge",
    )(o0, o1, o2, l0, l1, l2, y_b, ga, gb, x, mod_l, w_a, w_b, w_o)


def _ffn_kernel(x_ref, halo_ref, shift_ref, scale_ref, gate_ref, g_ref, wup_ref, cw_ref, cb_ref,
                wdn_ref, fin_ref, out_ref, he_s, act_s, *, final):
    i = pl.program_id(1)
    tm = x_ref.shape[1]
    hh = _norm_mod(halo_ref[0], g_ref[...], scale_ref[...], shift_ref[...]) * jnp.where(i > 0, 1.0, 0.0)
    he_s[0:tm] = _norm_mod(x_ref[0], g_ref[...], scale_ref[...], shift_ref[...]).astype(BF16)
    he_s[tm:] = hh.astype(BF16)

    def up(col):
        return jnp.dot(he_s[...], wup_ref[:, col:col + FFN_CHUNK], preferred_element_type=F32)

    def conv(u, col):
        w = cw_ref[:, col:col + FFN_CHUNK]
        return (w[0:1] * pltpu.roll(u, 2, 0)[:tm] + w[1:2] * pltpu.roll(u, 1, 0)[:tm]
                + (w[2:3] * u[:tm] + cb_ref[:, col:col + FFN_CHUNK]))

    chunks = list(range(0, D_FF, FFN_CHUNK))
    pending = [(up(c), up(D_FF + c)) for c in chunks[:FFN_LOOKAHEAD]]
    acc = None
    lo_c = 0
    group = -(-len(chunks) // FFN_DOWN_SPLITS)
    for n, c in enumerate(chunks):
        ug, uv = pending.pop(0)
        if n + FFN_LOOKAHEAD < len(chunks):
            ahead = chunks[n + FFN_LOOKAHEAD]
            pending.append((up(ahead), up(D_FF + ahead)))
        act_s[:, c:c + FFN_CHUNK] = (_silu(conv(ug, c)) * conv(uv, D_FF + c)).astype(BF16)
        if (n + 1) % group == 0 or n + 1 == len(chunks):
            hi_c = c + FFN_CHUNK
            part = jnp.dot(act_s[:, lo_c:hi_c], wdn_ref[lo_c:hi_c, :], preferred_element_type=F32)
            acc = part if acc is None else acc + part
            lo_c = hi_c
    y = x_ref[0] + gate_ref[...] * acc
    if final:
        ms = jnp.mean(y * y, axis=-1, keepdims=True)
        y = y * lax.rsqrt(ms + RMS_EPS) * fin_ref[...]
    out_ref[0] = y


def _ffn(x, mod_l, norm_g, w_up, conv_w, conv_b, w_down, final_g, final):
    batch, seq, d = x.shape
    tm = ROW_TILE
    row = pl.BlockSpec((1, tm, d), lambda b, i: (b, i, 0))
    halo = pl.BlockSpec((1, HALO, d), lambda b, i: (b, jnp.maximum(i * (tm // HALO) - 1, 0), 0))
    vec = lambda k: pl.BlockSpec((None, None, 1, d), lambda b, i, k=k: (b, k, 0, 0))
    full = lambda a: _resident(a.shape)
    norm_g = norm_g.reshape(1, d)
    conv_b = conv_b.reshape(1, -1)
    final_g = final_g.reshape(1, d)
    return pl.pallas_call(
        functools.partial(_ffn_kernel, final=final),
        out_shape=jax.ShapeDtypeStruct(x.shape, F32),
        grid=(batch, seq // tm),
        in_specs=[row, halo, vec(3), vec(4), vec(5), full(norm_g), full(w_up), full(conv_w),
                  full(conv_b), full(w_down), full(final_g)],
        out_specs=row,
        scratch_shapes=[pltpu.VMEM((tm + HALO, d), BF16), pltpu.VMEM((tm, D_FF), BF16)],
        compiler_params=_params(("arbitrary", "arbitrary")),
        name="conv_ffn",
    )(x, x, mod_l, mod_l, mod_l, norm_g, w_up, conv_w, conv_b, w_down, final_g)


def _rope_tables(seq):
    inv = 1.0 / (ROPE_THETA ** (jnp.arange(0, HEAD_DIM, 2, dtype=F32) / HEAD_DIM))
    ang = jnp.arange(seq, dtype=F32)[:, None] * inv[None, :]
    cos, sin = jnp.cos(ang), jnp.sin(ang)
    cos_t = jnp.tile(cos, (1, LANES // (HEAD_DIM // 2)))
    sin_t = jnp.tile(jnp.concatenate([-sin, sin], axis=1), (1, LANES // HEAD_DIM))
    return cos_t, sin_t


def kernel(x, c, norm1_g, norm2_g, final_g, w_mod, b_mod, w_in, cmp_pe_k, cmp_pe_v, cmp_w1_k, cmp_w2_k, cmp_w1_v, cmp_w2_v, w_br_a, w_br_b, w_out, w_up, conv_w, conv_b, w_down):
    batch, seq, d = x.shape
    depth = w_in.shape[0]
    cos_t, sin_t = _rope_tables(seq)
    mod = _modulation(c, w_mod, b_mod).reshape(depth, batch, N_MOD, 1, d)
    for layer in range(depth):
        mod_l = mod[layer]
        (qkv0, qkv1, qkv2, bqr, bqn, kc_raw, vc_raw, ksl, vsl, kwn, vwn, blog, ga, gb) = _input_projection(
            x, mod_l, norm1_g[layer], cos_t, sin_t, _pack_w_in(w_in[layer]))
        o_lse = [_banded_attention(qkv, dil) for qkv, (_, dil) in zip((qkv0, qkv1, qkv2), DIL_GROUPS)]

        pk = _pack_compress_weights(cmp_pe_k[layer], cmp_w1_k[layer], cmp_w2_k[layer])
        pv = _pack_compress_weights(cmp_pe_v[layer], cmp_w1_v[layer], cmp_w2_v[layer])
        pep, w1p, w2p = (jnp.stack([a, b]) for a, b in zip(pk, pv))
        kcd, vcd = _compress(kc_raw, vc_raw, pep, w1p, w2p)
        y_b = _nsa_attention(bqr, bqn, blog, kcd, vcd, ksl, vsl, kwn, vwn)

        x = _merge(o_lse, y_b, ga, gb, x, mod_l, w_br_a[layer].astype(BF16),
                   w_br_b[layer].astype(BF16), w_out[layer].astype(BF16))
        x = _ffn(x, mod_l, norm2_g[layer], w_up[layer].astype(BF16), conv_w[layer], conv_b[layer],
                 w_down[layer].astype(BF16), final_g, final=(layer == depth - 1))
    return x
```

```python
import functools

import jax
import jax.numpy as jnp
import numpy as np
from jax import lax
from jax.experimental import pallas as pl
from jax.experimental.pallas import tpu as pltpu

F32 = jnp.float32
BF16 = jnp.bfloat16

HEAD_DIM = 64
ROPE_THETA = 10000.0
RMS_EPS = 1e-6
NEG_INF = -1e30
FORCE_SCORE = 1e9
MASK_BIG = float(2.0 ** 100)
LOG2E = 1.4426950408889634
LN2 = 0.6931471805599453

DIL_GROUPS = ((128, 1), (512, 4), (2048, 16))
A_HEADS = 4
A_GROUP_WIDTH = A_HEADS * HEAD_DIM
BAND_BACK = 128
NSA_HEADS = 8
NSA_KV = 2
NSA_REP = NSA_HEADS // NSA_KV
NSA_Q_WIDTH = NSA_HEADS * HEAD_DIM
CMP_LEN = 32
CMP_STRIDE = 16
CMP_HID = 256
SEL_LEN = 64
SEL_TOPK = 16
WIN_LEN = 512
D_FF = 2816
CONV_W = 3
N_MOD = 6

LANES = 128
VMEM_LIMIT = 56 * 1024 * 1024

ROW_TILE = 1024
TAIL_TILE = 512
NSA_TILE = 256
BANDED_UNITS = 8
FFN_CHUNK = 256
FFN_LOOKAHEAD = 2
FFN_DOWN_SPLITS = 1
HALO = 8
V_ROWS = 80


def _sigmoid(x):
    return 0.5 * jnp.tanh(0.5 * x) + 0.5


def _silu(x):
    h = 0.5 * x
    return h + h * jnp.tanh(h)


def _params(semantics):
    return pltpu.CompilerParams(dimension_semantics=semantics, vmem_limit_bytes=VMEM_LIMIT)


def _resident(shape):
    nd = len(shape)
    return pl.BlockSpec(tuple(shape), lambda *_: (0,) * nd, pipeline_mode=pl.Buffered(1))


def _lane_iota(shape):
    return lax.broadcasted_iota(jnp.int32, shape, len(shape) - 1)


def _row_iota(shape):
    return lax.broadcasted_iota(jnp.int32, shape, len(shape) - 2)


def _mod_kernel(c_ref, w_ref, b_ref, o_ref):
    k = pl.program_id(1)
    part = jnp.dot(_silu(c_ref[...]), w_ref[0], preferred_element_type=F32)

    @pl.when(k == 0)
    def _():
        o_ref[0] = part + b_ref[0]

    @pl.when(k > 0)
    def _():
        o_ref[0] += part


def _modulation(c, w_mod, b_mod):
    depth, d, n = w_mod.shape
    batch = c.shape[0]
    tk = 256
    return pl.pallas_call(
        _mod_kernel,
        out_shape=jax.ShapeDtypeStruct((depth, batch, n), F32),
        grid=(depth, d // tk),
        in_specs=[
            pl.BlockSpec((batch, tk), lambda l, k: (0, k)),
            pl.BlockSpec((1, tk, n), lambda l, k: (l, k, 0)),
            pl.BlockSpec((1, 1, n), lambda l, k: (l, 0, 0)),
        ],
        out_specs=pl.BlockSpec((1, batch, n), lambda l, k: (l, 0, 0)),
        compiler_params=_params(("arbitrary", "arbitrary")),
        name="modulation",
    )(c, w_mod, b_mod.reshape(depth, 1, n))


def _norm_mod(x, g, scale, shift):
    ms = jnp.mean(x * x, axis=-1, keepdims=True)
    y = x * lax.rsqrt(ms + RMS_EPS) * g
    return y * (1.0 + scale) + shift


def _rope_chunk(xc, cos, sin_signed, first_half):
    partner = jnp.where(first_half, pltpu.roll(xc, LANES - HEAD_DIM // 2, 1),
                        pltpu.roll(xc, HEAD_DIM // 2, 1))
    return xc * cos + partner * sin_signed


_QKV_A = 3 * A_GROUP_WIDTH
_OFF_BQ = 3 * _QKV_A
_OFF_KV = _OFF_BQ + NSA_Q_WIDTH
_OFF_BLOG = _OFF_KV + 6 * LANES
_OFF_GA = _OFF_BLOG + LANES
N_IN_PACKED = _OFF_GA + 2 * 1024


def _inproj_kernel(x_ref, shift_ref, scale_ref, g_ref, cos_ref, sin_ref, w_ref,
                   qkv0_ref, qkv1_ref, qkv2_ref, bqr_ref, bqn_ref, kc_ref, vc_ref,
                   ksl_ref, vsl_ref, kwn_ref, vwn_ref, blog_ref, ga_ref, gb_ref, il_s):
    h = _norm_mod(x_ref[0], g_ref[...], scale_ref[...], shift_ref[...]).astype(BF16)
    cos = cos_ref[...]
    sin = sin_ref[...]
    rows = cos.shape[0]
    first_half = (_lane_iota((rows, LANES)) % HEAD_DIM) < (HEAD_DIM // 2)
    scale = HEAD_DIM ** -0.5 * LOG2E

    def proj(off, width):
        return jnp.dot(h, w_ref[:, off:off + width], preferred_element_type=F32)

    def rope(y, mult):
        parts = [_rope_chunk(y[:, c:c + LANES], cos, sin, first_half) * mult
                 for c in range(0, y.shape[1], LANES)]
        return parts[0] if len(parts) == 1 else jnp.concatenate(parts, axis=1)

    def store_strided(out_ref, y, dil, col):
        nblk = y.shape[1] // LANES
        width = out_ref.shape[2] // dil
        for j in range(nblk):
            il_s[j] = y[:, j * LANES:(j + 1) * LANES]
        for r in range(dil):
            for j in range(nblk):
                c = r * width + col + j * LANES
                out_ref[0, :, c:c + LANES] = il_s[j, pl.ds(r, rows // dil, stride=dil), :].astype(out_ref.dtype)

    for (_, dil), out_ref, base in zip(DIL_GROUPS, (qkv0_ref, qkv1_ref, qkv2_ref), (0, _QKV_A, 2 * _QKV_A)):
        pieces = (rope(proj(base, 256), scale), rope(proj(base + 256, 256), 1.0), proj(base + 512, 256))
        for n, y in enumerate(pieces):
            if dil == 1:
                out_ref[0, :, n * 256:(n + 1) * 256] = y.astype(out_ref.dtype)
            else:
                store_strided(out_ref, y, dil, n * 256)

    bq = proj(_OFF_BQ, NSA_Q_WIDTH)
    bqn_ref[0] = (bq * scale).astype(bqn_ref.dtype)
    bqr_ref[0] = rope(bq, scale).astype(bqr_ref.dtype)

    kv = proj(_OFF_KV, 6 * LANES)
    store_strided(kc_ref, kv[:, 0:128], CMP_STRIDE, 0)
    store_strided(vc_ref, kv[:, 128:256], CMP_STRIDE, 0)
    ksl_ref[0] = rope(kv[:, 256:384], 1.0).astype(ksl_ref.dtype)
    vsl_ref[0] = kv[:, 384:512].astype(vsl_ref.dtype)
    kwn_ref[0] = rope(kv[:, 512:640], 1.0).astype(kwn_ref.dtype)
    vwn_ref[0] = kv[:, 640:768].astype(vwn_ref.dtype)

    blog_ref[0] = proj(_OFF_BLOG, LANES)
    ga_ref[0] = proj(_OFF_GA, 1024).astype(ga_ref.dtype)
    gb_ref[0] = proj(_OFF_GA + 1024, 1024).astype(gb_ref.dtype)


def _pack_w_in(w):
    aq, ak, av = w[:, 0:768], w[:, 768:1536], w[:, 1536:2304]
    cols = []
    for g in range(3):
        sl = slice(g * 256, (g + 1) * 256)
        cols += [aq[:, sl], ak[:, sl], av[:, sl]]
    cols.append(w[:, 2304:3584])
    cols.append(jnp.pad(w[:, 3584:3608], ((0, 0), (0, LANES - 24))))
    cols.append(w[:, 3608:5656])
    return jnp.concatenate(cols, axis=1).astype(BF16)


def _input_projection(x, mod_l, norm_g, cos_t, sin_t, w_packed):
    batch, seq, d = x.shape
    tm = ROW_TILE
    row = lambda w, dil=1: pl.BlockSpec((1, tm // dil, dil * w), lambda b, i: (b, i, 0))
    vec = lambda k: pl.BlockSpec((None, None, 1, d), lambda b, i, k=k: (b, k, 0, 0))
    shp = lambda w, dt, dil=1: jax.ShapeDtypeStruct((batch, seq // dil, dil * w), dt)
    dils = [dil for _, dil in DIL_GROUPS]
    return pl.pallas_call(
        _inproj_kernel,
        out_shape=[shp(_QKV_A, BF16, dil) for dil in dils] + [shp(512, BF16)] * 2
        + [shp(128, BF16, CMP_STRIDE)] * 2 + [shp(128, F32)] * 4 + [shp(128, F32)] + [shp(1024, BF16)] * 2,
        grid=(batch, seq // tm),
        in_specs=[
            row(d), vec(0), vec(1),
            pl.BlockSpec((1, d), lambda b, i: (0, 0)),
            pl.BlockSpec((tm, LANES), lambda b, i: (i, 0)),
            pl.BlockSpec((tm, LANES), lambda b, i: (i, 0)),
            _resident((d, N_IN_PACKED)),
        ],
        out_specs=[row(_QKV_A, dil) for dil in dils] + [row(512)] * 2 + [row(128, CMP_STRIDE)] * 2
        + [row(128)] * 5 + [row(1024)] * 2,
        scratch_shapes=[pltpu.VMEM((A_GROUP_WIDTH // LANES, tm, LANES), F32)],
        compiler_params=_params(("arbitrary", "arbitrary")),
        name="input_projection",
    )(x, mod_l, mod_l, norm_g.reshape(1, d), cos_t, sin_t, w_packed)


def _banded_kernel(qkv_ref, o_ref, lse_ref, *, length, nres):
    blk = 128
    nq = max(1, min(BANDED_UNITS // (2 * nres), length // blk))
    nkeys = 2 * blk if length > blk else blk
    lane = _lane_iota((blk, LANES))
    lo = lane < HEAD_DIM
    qrow = _row_iota((2 * blk, nkeys)) % blk
    kcol = _lane_iota((2 * blk, nkeys))

    def rows_of(i, qb):
        r0 = pl.multiple_of((i * nq + qb) * blk, blk)
        ks = pl.multiple_of(jnp.maximum(r0 - blk, 0), blk) if length > blk else 0
        return r0, ks

    def score(i, qb, res, pair):
        base = res * _QKV_A + pair * LANES
        r0, ks = rows_of(i, qb)
        qp = qkv_ref[0, pl.ds(r0, blk), base:base + LANES]
        k2 = qkv_ref[0, pl.ds(ks, nkeys), base + 256:base + 256 + LANES]
        zero = jnp.zeros_like(qp)
        qs = jnp.concatenate([jnp.where(lo, qp, zero), jnp.where(lo, zero, qp)], axis=0)
        s = lax.dot_general(qs, k2, (((1,), (1,)), ((), ())), preferred_element_type=F32)
        dist = (r0 + qrow) - (ks + kcol)
        return jnp.where((dist >= 0) & (dist <= BAND_BACK), s, NEG_INF)

    def soft(s):
        m = jnp.max(s, axis=-1, keepdims=True)
        e = jnp.exp2(s - m)
        den = jnp.sum(e, axis=-1, keepdims=True)
        return e.astype(BF16), 1.0 / den, m * LN2 + jnp.log(den)

    def emit(i, qb, res, pair, e, inv, lse):
        base = res * _QKV_A + pair * LANES
        r0, ks = rows_of(i, qb)
        v2 = qkv_ref[0, pl.ds(ks, nkeys), base + 512:base + 512 + LANES]
        o = jnp.dot(e, v2, preferred_element_type=F32) * inv
        ob = res * A_GROUP_WIDTH + pair * LANES
        o_ref[0, pl.ds(r0, blk), ob:ob + LANES] = jnp.where(lo, o[:blk], o[blk:]).astype(o_ref.dtype)
        lse_ref[0, pl.ds(r0, blk), ob:ob + LANES] = jnp.where(
            lo, jnp.broadcast_to(lse[:blk], (blk, LANES)), jnp.broadcast_to(lse[blk:], (blk, LANES)))

    units = [(qb, res, pair) for qb in range(nq) for res in range(nres) for pair in range(2)]

    def body(i, carry):
        ss = [score(i, *u) for u in units]
        ps = [soft(s) for s in ss]
        for u, p in zip(units, ps):
            emit(i, *u, *p)
        return carry

    lax.fori_loop(0, length // (blk * nq), body, 0)


def _banded_attention(view, dil):
    batch, length, _ = view.shape
    nres = min(dil, 4)
    return pl.pallas_call(
        functools.partial(_banded_kernel, length=length, nres=nres),
        out_shape=[jax.ShapeDtypeStruct((batch, length, dil * A_GROUP_WIDTH), BF16),
                   jax.ShapeDtypeStruct((batch, length, dil * A_GROUP_WIDTH), F32)],
        grid=(batch, dil // nres),
        in_specs=[pl.BlockSpec((1, length, nres * _QKV_A), lambda b, r: (b, 0, r))],
        out_specs=[pl.BlockSpec((1, length, nres * A_GROUP_WIDTH), lambda b, r: (b, 0, r))] * 2,
        compiler_params=_params(("arbitrary", "arbitrary")),
        name=f"banded_attention_d{dil}",
    )(view)


def _compress_kernel(k_ref, v_ref, pe_ref, w1_ref, w2_ref, kc_ref, vc_ref):
    nchunk = k_ref.shape[1]
    for t, (src, dst) in enumerate(((k_ref, kc_ref), (v_ref, vc_ref))):
        c = src[0].astype(F32)
        top = jnp.dot((c + pe_ref[t, 0]).astype(BF16), w1_ref[t, 0], preferred_element_type=F32)
        bot = jnp.dot((c + pe_ref[t, 1]).astype(BF16), w1_ref[t, 1], preferred_element_type=F32)
        pre = top + pltpu.roll(bot, nchunk - 1, 0)
        for g in range(NSA_KV):
            hid = _silu(pre[:, g * CMP_HID:(g + 1) * CMP_HID]).astype(BF16)
            res = jnp.dot(hid, w2_ref[t], preferred_element_type=F32)
            dst[0, g] = (res if t == 0 else res.T).astype(dst.dtype)


def _pack_compress_weights(pe, w1, w2):
    half = CMP_LEN // 2
    w1r = w1.reshape(2, half, HEAD_DIM, CMP_HID)
    zero = jnp.zeros_like(w1r)
    w1p = jnp.stack([jnp.concatenate([w1r, zero], axis=-1), jnp.concatenate([zero, w1r], axis=-1)], axis=2)
    w1p = w1p.reshape(2, half * LANES, NSA_KV * CMP_HID)
    pep = jnp.tile(pe.reshape(2, half, 1, HEAD_DIM), (1, 1, NSA_KV, 1)).reshape(2, 1, half * LANES)
    w2p = jnp.concatenate([w2, w2], axis=1)
    return pep, w1p.astype(BF16), w2p.astype(BF16)


def _compress(kc_raw, vc_raw, pep, w1p, w2p):
    batch, nchunk, width = kc_raw.shape
    view = lambda a: a
    full = lambda a: _resident(a.shape)
    out = jax.ShapeDtypeStruct((batch, NSA_KV, nchunk, LANES), BF16)
    return pl.pallas_call(
        _compress_kernel,
        out_shape=[out, out],
        grid=(batch,),
        in_specs=[pl.BlockSpec((1, nchunk, width), lambda b: (b, 0, 0))] * 2
        + [full(pep), full(w1p), full(w2p)],
        out_specs=[pl.BlockSpec((1, NSA_KV, nchunk, LANES), lambda b: (b, 0, 0, 0))] * 2,
        compiler_params=_params(("arbitrary",)),
        name="nsa_compress",
    )(view(kc_raw), view(vc_raw), pep, w1p, w2p)


def _nsa_kernel(bqr_ref, bqn_ref, blog_ref, kcd_ref, vct_ref, ksl_ref, vsl_ref, kwn_ref, vwn_ref,
                ovl_ref, y_ref, ksel_s, vsel_s, kwin_s, vwin_s, m_s, acc_s, alpha_s, s_s, p_s, *, tile, seq):
    qi = pl.program_id(1)
    q0 = qi * tile
    cols = NSA_REP * tile
    nblk = seq // SEL_LEN
    ncmp = kcd_ref.shape[2]
    ntile = seq // tile
    nt = (((1,), (1,)), ((), ()))

    @pl.when(qi == 0)
    def _build_kv():
        lane = _lane_iota((seq, LANES))
        lo = lane < HEAD_DIM
        blk_of_row = _row_iota((seq, LANES)) // SEL_LEN
        selmask = jnp.where(blk_of_row == lane - HEAD_DIM, -MASK_BIG, 0.0)
        for src, dst, fill in ((ksl_ref, ksel_s, selmask), (kwn_ref, kwin_s, 0.0)):
            a = src[0]
            dst[0] = jnp.where(lo, a, fill).astype(dst.dtype)
            dst[1] = jnp.where(lo, pltpu.roll(a, HEAD_DIM, 1), fill).astype(dst.dtype)
        ones_rows = jnp.where(_row_iota((V_ROWS - HEAD_DIM, seq)) == 0, 1.0, 0.0)
        for src, dst in ((vsl_ref, vsel_s), (vwn_ref, vwin_s)):
            vt = src[0].T
            for g in range(NSA_KV):
                full = jnp.concatenate([vt[g * HEAD_DIM:(g + 1) * HEAD_DIM], ones_rows], axis=0).astype(dst.dtype)
                for kt in range(ntile):
                    dst[g, kt] = full[:, kt * tile:(kt + 1) * tile]

    lo_t = _lane_iota((tile, LANES)) < HEAD_DIM

    def stack_heads(q_ref, g, extra):
        out = []
        for pair in range(2):
            c = g * 256 + pair * LANES
            qp = q_ref[0, :, c:c + LANES].astype(F32)
            out.append(jnp.where(lo_t, qp, extra))
            out.append(jnp.where(lo_t, pltpu.roll(qp, HEAD_DIM, 1), extra))
        return jnp.concatenate(out, axis=0).astype(BF16)

    qpos = q0 + (_lane_iota((tile, cols)) % tile)
    krow = _row_iota((tile, cols))

    def scores(k_s, g, kt, qs, slot):
        k0 = pl.multiple_of(kt * tile, tile)
        s_s[g, slot] = lax.dot_general(k_s[g, pl.ds(k0, tile), :], qs, nt, preferred_element_type=F32)

    def softmax(g, slot, keep):
        s = s_s[g, slot]
        if keep is not None:
            s = jnp.where(keep, s, NEG_INF)
        m_old = m_s[g]
        m_new = jnp.maximum(m_old, jnp.max(s, axis=0, keepdims=True))
        alpha_s[g, slot] = jnp.exp2(m_old - m_new)
        p_s[g, slot] = jnp.exp2(s - m_new).astype(BF16)
        m_s[g] = m_new

    def pv(v_s, g, kt, slot):
        acc_s[g] = alpha_s[g, slot] * acc_s[g] + jnp.dot(v_s[g, kt], p_s[g, slot], preferred_element_type=F32)

    def flash_init(g):
        m_s[g] = jnp.full(m_s.shape[1:], NEG_INF, F32)
        acc_s[g] = jnp.zeros(acc_s.shape[1:], F32)

    def flash_result(g):
        acc = acc_s[g]
        return acc[0:HEAD_DIM] * (1.0 / acc[HEAD_DIM:HEAD_DIM + 1])

    gates = _sigmoid(blog_ref[0].T)
    zero_t = jnp.zeros((tile, LANES), F32)
    blk = _row_iota((nblk, tile))
    tpos = q0 + _lane_iota((nblk, tile))
    groups = range(NSA_KV)
    causal = qpos >= q0 + krow

    def cmp_scores(g):
        qn = stack_heads(bqn_ref, g, zero_t)
        return lax.dot_general(kcd_ref[0, g], qn, nt, preferred_element_type=F32)

    def cmp_probs(s):
        tq = q0 + (_lane_iota((ncmp, cols)) % tile)
        cvis = (_row_iota((ncmp, cols)) * CMP_STRIDE + (CMP_LEN - 1)) <= tq
        s = jnp.where(cvis, s, NEG_INF)
        m = jnp.max(s, axis=0, keepdims=True)
        e = jnp.where(cvis, jnp.exp2(s - m), 0.0)
        den = jnp.sum(e, axis=0, keepdims=True)
        return e * jnp.where(den > 0.0, 1.0 / den, 0.0)

    def cmp_products(g, p):
        o_cmp = jnp.dot(vct_ref[0, g, 0:HEAD_DIM, :], p.astype(BF16), preferred_element_type=F32)
        psum = p[:, 0:tile] + p[:, tile:2 * tile] + p[:, 2 * tile:3 * tile] + p[:, 3 * tile:4 * tile]
        p_hi = psum.astype(BF16)
        p_lo = (psum - p_hi.astype(F32)).astype(BF16)
        ovl = ovl_ref[...]
        imp = (jnp.dot(ovl, p_hi, preferred_element_type=F32)
               + jnp.dot(ovl, p_lo, preferred_element_type=F32))
        return o_cmp, imp

    def select_queries(g, imp):
        cur = tpos // SEL_LEN
        forced = (blk == 0) | (blk == cur) | (blk == cur - 1)
        imp = jnp.where(forced, FORCE_SCORE, imp)
        imp = jnp.where(blk * SEL_LEN <= tpos, imp, NEG_INF)
        rank = jnp.zeros((nblk, tile), F32)
        for k in range(nblk):
            vk = imp[k:k + 1, :]
            tie = jnp.where(blk > k, 1.0, 0.0)
            rank = rank + jnp.where(vk > imp, 1.0, jnp.where(vk == imp, tie, 0.0))
        notsel = jnp.where(rank >= SEL_TOPK, 1.0, 0.0)
        extra = jnp.concatenate([jnp.zeros((HEAD_DIM, tile), F32), notsel,
                                 jnp.zeros((LANES - HEAD_DIM - nblk, tile), F32)], axis=0).T.astype(BF16)
        return q_win[g] + jnp.concatenate([extra] * NSA_REP, axis=0)

    far, near = qi - 2, qi - 1
    keep_far = (qpos - (far * tile + krow)) <= jnp.where(far >= 0, WIN_LEN - 1, -1)
    keep_near = krow >= jnp.where(near >= 0, 0, tile)
    q_win = [stack_heads(bqr_ref, g, zero_t) for g in groups]
    for g in groups:
        flash_init(g)
        scores(kwin_s, g, jnp.maximum(far, 0), q_win[g], 0)
    for g in groups:
        scores(kwin_s, g, jnp.maximum(near, 0), q_win[g], 1)
    cmp_s = [cmp_scores(g) for g in groups]
    for g in groups:
        softmax(g, 0, keep_far)
    cmp_p = [cmp_probs(s) for s in cmp_s]
    for g in groups:
        scores(kwin_s, g, qi, q_win[g], 0)
    for g in groups:
        pv(vwin_s, g, jnp.maximum(far, 0), 0)
    o_cmps, imps = zip(*[cmp_products(g, cmp_p[g]) for g in groups])
    for g in groups:
        softmax(g, 1, keep_near)
    for g in groups:
        pv(vwin_s, g, jnp.maximum(near, 0), 1)
    q_sel = [select_queries(g, imps[g]) for g in groups]
    for g in groups:
        softmax(g, 0, causal)
    for g in groups:
        pv(vwin_s, g, qi, 0)
    o_wins = [flash_result(g) for g in groups]

    odd = qi & 1
    for g in groups:
        flash_init(g)

    @pl.when(odd == 1)
    def _():
        for g in groups:
            scores(ksel_s, g, 0, q_sel[g], 0)
        for g in groups:
            softmax(g, 0, None)
        for g in groups:
            pv(vsel_s, g, 0, 0)

    for g in groups:
        p_s[g, 1] = jnp.zeros(p_s.shape[2:], BF16)
        alpha_s[g, 1] = jnp.ones(alpha_s.shape[2:], F32)
        scores(ksel_s, g, odd, q_sel[g], 0)

    def stage(j, slot):
        for g in groups:
            scores(ksel_s, g, j + 1, q_sel[g], 1 - slot)
        for g in groups:
            pv(vsel_s, g, jnp.maximum(j - 1, 0), 1 - slot)
        for g in groups:
            softmax(g, slot, None)

    def sel_body(jj, carry):
        j = odd + 2 * jj
        stage(j, 0)
        stage(j + 1, 1)
        return carry

    lax.fori_loop(0, (qi - odd) // 2, sel_body, 0)
    for g in groups:
        pv(vsel_s, g, jnp.maximum(qi - 1, 0), 1)
    for g in groups:
        softmax(g, 0, causal)
    for g in groups:
        pv(vsel_s, g, qi, 0)
    o_sels = [flash_result(g) for g in groups]

    heads = []
    for g in groups:
        for r in range(NSA_REP):
            c = (g * NSA_REP + r) * 3
            sl = slice(r * tile, (r + 1) * tile)
            heads.append(gates[c:c + 1] * o_cmps[g][:, sl] + gates[c + 1:c + 2] * o_sels[g][:, sl]
                         + gates[c + 2:c + 3] * o_wins[g][:, sl])
    y_ref[0] = jnp.concatenate(heads, axis=0).T.astype(y_ref.dtype)


def _overlap_matrix(seq):
    starts = np.arange(seq // CMP_STRIDE) * CMP_STRIDE
    bstart = np.arange(seq // SEL_LEN) * SEL_LEN
    ovl = (starts[None, :] < bstart[:, None] + SEL_LEN) & (starts[None, :] + CMP_LEN > bstart[:, None])
    return jnp.asarray(ovl, dtype=BF16)


def _nsa_attention(bqr, bqn, blog, kcd, vct, ksl, vsl, kwn, vwn):
    batch, seq, _ = bqr.shape
    tile = NSA_TILE
    assert 2 * tile == WIN_LEN
    cols = NSA_REP * tile
    ncmp = seq // CMP_STRIDE
    nblk = seq // SEL_LEN
    qspec = lambda w: pl.BlockSpec((1, tile, w), lambda b, i: (b, i, 0))
    kvspec = pl.BlockSpec((1, seq, LANES), lambda b, i: (b, 0, 0))
    cspec = pl.BlockSpec((1, NSA_KV, ncmp, LANES), lambda b, i: (b, 0, 0, 0))
    k_scratch = pltpu.VMEM((NSA_KV, seq, LANES), BF16)
    v_scratch = pltpu.VMEM((NSA_KV, seq // tile, V_ROWS, tile), BF16)
    return pl.pallas_call(
        functools.partial(_nsa_kernel, tile=tile, seq=seq),
        out_shape=jax.ShapeDtypeStruct((batch, seq, NSA_Q_WIDTH), BF16),
        grid=(batch, seq // tile),
        in_specs=[qspec(NSA_Q_WIDTH), qspec(NSA_Q_WIDTH), qspec(LANES), cspec, cspec,
                  kvspec, kvspec, kvspec, kvspec,
                  pl.BlockSpec((nblk, ncmp), lambda b, i: (0, 0))],
        out_specs=qspec(NSA_Q_WIDTH),
        scratch_shapes=[k_scratch, v_scratch, k_scratch, v_scratch,
                        pltpu.VMEM((NSA_KV, 1, cols), F32), pltpu.VMEM((NSA_KV, V_ROWS, cols), F32),
                        pltpu.VMEM((NSA_KV, 2, 1, cols), F32), pltpu.VMEM((NSA_KV, 2, tile, cols), F32),
                        pltpu.VMEM((NSA_KV, 2, tile, cols), BF16)],
        compiler_params=_params(("arbitrary", "arbitrary")),
        name="nsa_attention",
    )(bqr, bqn, blog, kcd, vct, ksl, vsl, kwn, vwn, _overlap_matrix(seq))


def _merge_ffn_kernel(o0_ref, o1_ref, o2_ref, l0_ref, l1_ref, l2_ref, yb_ref, ga_ref, gb_ref, x_ref,
                      gate_ref, wa_ref, wb_ref, wo_ref, shift2_ref, scale2_ref, gate2_ref, g2_ref,
                      wup_ref, cw_ref, cb_ref, wdn_ref, fin_ref, out_ref, il_s, he_s, act_s, carry_s,
                      *, final):
    i = pl.program_id(1)
    tm = x_ref.shape[1]

    def natural(ref, dil, slot):
        if dil == 1:
            return ref[0].astype(F32)
        nblk = A_GROUP_WIDTH // LANES
        for r in range(dil):
            for j in range(nblk):
                c = r * A_GROUP_WIDTH + j * LANES
                il_s[slot, j, pl.ds(r, tm // dil, stride=dil), :] = ref[0, :, c:c + LANES].astype(F32)
        return jnp.concatenate([il_s[slot, j] for j in range(nblk)], axis=1)

    dils = [dil for _, dil in DIL_GROUPS]
    o0, o1, o2 = (natural(r, dil, n) for n, (r, dil) in enumerate(zip((o0_ref, o1_ref, o2_ref), dils)))
    l0, l1, l2 = (natural(r, dil, 3 + n) for n, (r, dil) in enumerate(zip((l0_ref, l1_ref, l2_ref), dils)))
    m = jnp.maximum(jnp.maximum(l0, l1), l2)
    e0, e1, e2 = jnp.exp(l0 - m), jnp.exp(l1 - m), jnp.exp(l2 - m)
    inv = 1.0 / (e0 + e1 + e2)
    y_a = (e0 * inv) * o0 + (e1 * inv) * o1 + (e2 * inv) * o2
    pa = jnp.dot(y_a.astype(BF16), wa_ref[...], preferred_element_type=F32)
    pb = jnp.dot(yb_ref[0], wb_ref[...], preferred_element_type=F32)
    merged = _sigmoid(ga_ref[0].astype(F32)) * pa + _sigmoid(gb_ref[0].astype(F32)) * pb
    out = jnp.dot(merged.astype(BF16), wo_ref[...], preferred_element_type=F32)
    out_ref[0] = x_ref[0] + gate_ref[...] * out

    @pl.when(i == 0)
    def _():
        carry_s[...] = jnp.zeros(carry_s.shape, F32)

    hx = _norm_mod(out_ref[0], g2_ref[...], scale2_ref[...], shift2_ref[...])
    he_s[0:tm] = hx.astype(BF16)
    he_s[tm:] = carry_s[...].astype(BF16)
    carry_s[...] = hx[tm - HALO:]

    def up(col):
        return jnp.dot(he_s[...], wup_ref[:, col:col + FFN_CHUNK], preferred_element_type=F32)

    def conv(u, col):
        w = cw_ref[:, col:col + FFN_CHUNK]
        return (w[0:1] * pltpu.roll(u, 2, 0)[:tm] + w[1:2] * pltpu.roll(u, 1, 0)[:tm]
                + (w[2:3] * u[:tm] + cb_ref[:, col:col + FFN_CHUNK]))

    chunks = list(range(0, D_FF, FFN_CHUNK))
    pending = [(up(c), up(D_FF + c)) for c in chunks[:FFN_LOOKAHEAD]]
    acc = None
    lo_c = 0
    group = -(-len(chunks) // FFN_DOWN_SPLITS)
    for n, c in enumerate(chunks):
        ug, uv = pending.pop(0)
        if n + FFN_LOOKAHEAD < len(chunks):
            ahead = chunks[n + FFN_LOOKAHEAD]
            pending.append((up(ahead), up(D_FF + ahead)))
        act_s[:, c:c + FFN_CHUNK] = (_silu(conv(ug, c)) * conv(uv, D_FF + c)).astype(BF16)
        if (n + 1) % group == 0 or n + 1 == len(chunks):
            hi_c = c + FFN_CHUNK
            part = jnp.dot(act_s[:, lo_c:hi_c], wdn_ref[lo_c:hi_c, :], preferred_element_type=F32)
            acc = part if acc is None else acc + part
            lo_c = hi_c
    y = out_ref[0] + gate2_ref[...] * acc
    if final:
        ms = jnp.mean(y * y, axis=-1, keepdims=True)
        y = y * lax.rsqrt(ms + RMS_EPS) * fin_ref[...]
    out_ref[0] = y


def _merge_ffn(o_lse, y_b, ga, gb, x, mod_l, w_a, w_b, w_o, norm_g, w_up, conv_w, conv_b, w_down,
               final_g, final):
    batch, seq, d = x.shape
    tm = TAIL_TILE
    row = lambda w, dil=1: pl.BlockSpec((1, tm // dil, dil * w), lambda b, i: (b, i, 0))
    vec = lambda k: pl.BlockSpec((None, None, 1, d), lambda b, i, k=k: (b, k, 0, 0))
    full = lambda a: _resident(a.shape)
    (o0, l0), (o1, l1), (o2, l2) = o_lse
    grp = [row(A_GROUP_WIDTH, dil) for _, dil in DIL_GROUPS]
    norm_g = norm_g.reshape(1, d)
    conv_b = conv_b.reshape(1, -1)
    final_g = final_g.reshape(1, d)
    return pl.pallas_call(
        functools.partial(_merge_ffn_kernel, final=final),
        out_shape=jax.ShapeDtypeStruct(x.shape, F32),
        grid=(batch, seq // tm),
        in_specs=grp + grp + [row(NSA_Q_WIDTH), row(d), row(d), row(d), vec(2),
                              full(w_a), full(w_b), full(w_o), vec(3), vec(4), vec(5), full(norm_g),
                              full(w_up), full(conv_w), full(conv_b), full(w_down), full(final_g)],
        out_specs=row(d),
        scratch_shapes=[pltpu.VMEM((6, A_GROUP_WIDTH // LANES, tm, LANES), F32),
                        pltpu.VMEM((tm + HALO, d), BF16), pltpu.VMEM((tm, D_FF), BF16),
                        pltpu.VMEM((HALO, d), F32)],
        compiler_params=_params(("arbitrary", "arbitrary")),
        name="merge_ffn",
    )(o0, o1, o2, l0, l1, l2, y_b, ga, gb, x, mod_l, w_a, w_b, w_o, mod_l, mod_l, mod_l, norm_g,
      w_up, conv_w, conv_b, w_down, final_g)


def _rope_tables(seq):
    inv = 1.0 / (ROPE_THETA ** (jnp.arange(0, HEAD_DIM, 2, dtype=F32) / HEAD_DIM))
    ang = jnp.arange(seq, dtype=F32)[:, None] * inv[None, :]
    cos, sin = jnp.cos(ang), jnp.sin(ang)
    cos_t = jnp.tile(cos, (1, LANES // (HEAD_DIM // 2)))
    sin_t = jnp.tile(jnp.concatenate([-sin, sin], axis=1), (1, LANES // HEAD_DIM))
    return cos_t, sin_t


def kernel(x, c, norm1_g, norm2_g, final_g, w_mod, b_mod, w_in, cmp_pe_k, cmp_pe_v, cmp_w1_k, cmp_w2_k, cmp_w1_v, cmp_w2_v, w_br_a, w_br_b, w_out, w_up, conv_w, conv_b, w_down):
    batch, seq, d = x.shape
    depth = w_in.shape[0]
    cos_t, sin_t = _rope_tables(seq)
    mod = _modulation(c, w_mod, b_mod).reshape(depth, batch, N_MOD, 1, d)
    for layer in range(depth):
        mod_l = mod[layer]
        (qkv0, qkv1, qkv2, bqr, bqn, kc_raw, vc_raw, ksl, vsl, kwn, vwn, blog, ga, gb) = _input_projection(
            x, mod_l, norm1_g[layer], cos_t, sin_t, _pack_w_in(w_in[layer]))
        o_lse = [_banded_attention(qkv, dil) for qkv, (_, dil) in zip((qkv0, qkv1, qkv2), DIL_GROUPS)]

        pk = _pack_compress_weights(cmp_pe_k[layer], cmp_w1_k[layer], cmp_w2_k[layer])
        pv = _pack_compress_weights(cmp_pe_v[layer], cmp_w1_v[layer], cmp_w2_v[layer])
        pep, w1p, w2p = (jnp.stack([a, b]) for a, b in zip(pk, pv))
        kcd, vcd = _compress(kc_raw, vc_raw, pep, w1p, w2p)
        y_b = _nsa_attention(bqr, bqn, blog, kcd, vcd, ksl, vsl, kwn, vwn)

        x = _merge_ffn(o_lse, y_b, ga, gb, x, mod_l, w_br_a[layer].astype(BF16),
                       w_br_b[layer].astype(BF16), w_out[layer].astype(BF16), norm2_g[layer],
                       w_up[layer].astype(BF16), conv_w[layer], conv_b[layer],
                       w_down[layer].astype(BF16), final_g, final=(layer == depth - 1))
    return x
```

```python
import functools

import jax
import jax.numpy as jnp
import numpy as np
from jax import lax
from jax.experimental import pallas as pl
from jax.experimental.pallas import tpu as pltpu

F32 = jnp.float32
BF16 = jnp.bfloat16

HEAD_DIM = 64
ROPE_THETA = 10000.0
RMS_EPS = 1e-6
NEG_INF = -1e30
FORCE_SCORE = 1e9
MASK_BIG = float(2.0 ** 100)
LOG2E = 1.4426950408889634
LN2 = 0.6931471805599453

DIL_GROUPS = ((128, 1), (512, 4), (2048, 16))
A_HEADS = 4
A_GROUP_WIDTH = A_HEADS * HEAD_DIM
BAND_BACK = 128
NSA_HEADS = 8
NSA_KV = 2
NSA_REP = NSA_HEADS // NSA_KV
NSA_Q_WIDTH = NSA_HEADS * HEAD_DIM
CMP_LEN = 32
CMP_STRIDE = 16
CMP_HID = 256
SEL_LEN = 64
SEL_TOPK = 16
WIN_LEN = 512
D_FF = 2816
CONV_W = 3
N_MOD = 6

LANES = 128
VMEM_LIMIT = 56 * 1024 * 1024

ROW_TILE = 1024
TAIL_TILE = 512
NSA_TILE = 256
BANDED_UNITS = 8
FFN_CHUNK = 256
FFN_LOOKAHEAD = 2
FFN_DOWN_SPLITS = 1
HALO = 8
V_ROWS = 80


def _sigmoid(x):
    return 0.5 * jnp.tanh(0.5 * x) + 0.5


def _silu(x):
    h = 0.5 * x
    return h + h * jnp.tanh(h)


def _params(semantics):
    return pltpu.CompilerParams(dimension_semantics=semantics, vmem_limit_bytes=VMEM_LIMIT)


def _resident(shape):
    nd = len(shape)
    return pl.BlockSpec(tuple(shape), lambda *_: (0,) * nd, pipeline_mode=pl.Buffered(1))


def _lane_iota(shape):
    return lax.broadcasted_iota(jnp.int32, shape, len(shape) - 1)


def _row_iota(shape):
    return lax.broadcasted_iota(jnp.int32, shape, len(shape) - 2)


def _mod_kernel(c_ref, w_ref, b_ref, o_ref):
    k = pl.program_id(1)
    part = jnp.dot(_silu(c_ref[...]), w_ref[0], preferred_element_type=F32)

    @pl.when(k == 0)
    def _():
        o_ref[0] = part + b_ref[0]

    @pl.when(k > 0)
    def _():
        o_ref[0] += part


def _modulation(c, w_mod, b_mod):
    depth, d, n = w_mod.shape
    batch = c.shape[0]
    tk = 256
    return pl.pallas_call(
        _mod_kernel,
        out_shape=jax.ShapeDtypeStruct((depth, batch, n), F32),
        grid=(depth, d // tk),
        in_specs=[
            pl.BlockSpec((batch, tk), lambda l, k: (0, k)),
            pl.BlockSpec((1, tk, n), lambda l, k: (l, k, 0)),
            pl.BlockSpec((1, 1, n), lambda l, k: (l, 0, 0)),
        ],
        out_specs=pl.BlockSpec((1, batch, n), lambda l, k: (l, 0, 0)),
        compiler_params=_params(("arbitrary", "arbitrary")),
        name="modulation",
    )(c, w_mod, b_mod.reshape(depth, 1, n))


def _norm_mod(x, g, scale, shift):
    ms = jnp.mean(x * x, axis=-1, keepdims=True)
    y = x * lax.rsqrt(ms + RMS_EPS) * g
    return y * (1.0 + scale) + shift


def _rope_chunk(xc, cos, sin_signed, first_half):
    partner = jnp.where(first_half, pltpu.roll(xc, LANES - HEAD_DIM // 2, 1),
                        pltpu.roll(xc, HEAD_DIM // 2, 1))
    return xc * cos + partner * sin_signed


_QKV_A = 3 * A_GROUP_WIDTH
_A_WIDTH = 3 * A_GROUP_WIDTH
_OFF_BQ = 3 * _A_WIDTH
_OFF_KV = _OFF_BQ + NSA_Q_WIDTH
_OFF_TAIL = _OFF_KV + 6 * LANES
_N_BLOG = NSA_HEADS * 3
_TAIL_WIDTH = 17 * LANES
N_IN_PACKED = _OFF_TAIL + _TAIL_WIDTH


def _inproj_kernel(x_ref, shift_ref, scale_ref, g_ref, cos_ref, sin_ref, w_ref,
                   qkv0_ref, qkv1_ref, qkv2_ref, bqr_ref, bqn_ref, kc_ref, vc_ref,
                   ksl_ref, vsl_ref, kwn_ref, vwn_ref, blog_ref, ga_ref, gb_ref, il_s):
    h = _norm_mod(x_ref[0], g_ref[...], scale_ref[...], shift_ref[...]).astype(BF16)
    cos = cos_ref[...]
    sin = sin_ref[...]
    rows = cos.shape[0]
    first_half = (_lane_iota((rows, LANES)) % HEAD_DIM) < (HEAD_DIM // 2)
    scale = HEAD_DIM ** -0.5 * LOG2E

    def proj(off, width):
        return jnp.dot(h, w_ref[:, off:off + width], preferred_element_type=F32)

    def rope(y, mult):
        parts = [_rope_chunk(y[:, c:c + LANES], cos, sin, first_half) * mult
                 for c in range(0, y.shape[1], LANES)]
        return parts[0] if len(parts) == 1 else jnp.concatenate(parts, axis=1)

    def store_strided(out_ref, y, dil, col):
        nblk = y.shape[1] // LANES
        width = out_ref.shape[2] // dil
        for j in range(nblk):
            il_s[j] = y[:, j * LANES:(j + 1) * LANES]
        for r in range(dil):
            for j in range(nblk):
                c = r * width + col + j * LANES
                out_ref[0, :, c:c + LANES] = il_s[j, pl.ds(r, rows // dil, stride=dil), :].astype(out_ref.dtype)

    for g, ((_, dil), out_ref) in enumerate(zip(DIL_GROUPS, (qkv0_ref, qkv1_ref, qkv2_ref))):
        base = g * A_GROUP_WIDTH
        pieces = (rope(proj(base, 256), scale), rope(proj(_A_WIDTH + base, 256), 1.0),
                  proj(2 * _A_WIDTH + base, 256))
        for n, y in enumerate(pieces):
            if dil == 1:
                out_ref[0, :, n * 256:(n + 1) * 256] = y.astype(out_ref.dtype)
            else:
                store_strided(out_ref, y, dil, n * 256)

    bq = proj(_OFF_BQ, NSA_Q_WIDTH)
    bqn_ref[0] = (bq * scale).astype(bqn_ref.dtype)
    bqr_ref[0] = rope(bq, scale).astype(bqr_ref.dtype)

    kv = proj(_OFF_KV, 6 * LANES)
    store_strided(kc_ref, kv[:, 0:128], CMP_STRIDE, 0)
    store_strided(vc_ref, kv[:, 128:256], CMP_STRIDE, 0)
    ksl_ref[0] = rope(kv[:, 256:384], 1.0).astype(ksl_ref.dtype)
    vsl_ref[0] = kv[:, 384:512].astype(vsl_ref.dtype)
    kwn_ref[0] = rope(kv[:, 512:640], 1.0).astype(kwn_ref.dtype)
    vwn_ref[0] = kv[:, 640:768].astype(vwn_ref.dtype)

    tail = proj(_OFF_TAIL, _TAIL_WIDTH)
    blog_ref[0] = tail[:, 0:LANES]
    ga_ref[0] = tail[:, _N_BLOG:_N_BLOG + 1024].astype(ga_ref.dtype)
    gb_ref[0] = tail[:, _N_BLOG + 1024:_N_BLOG + 2048].astype(gb_ref.dtype)


def _pack_w_in(w):
    return jnp.pad(w, ((0, 0), (0, N_IN_PACKED - w.shape[1]))).astype(BF16)


def _input_projection(x, mod_l, norm_g, cos_t, sin_t, w_packed):
    batch, seq, d = x.shape
    tm = ROW_TILE
    row = lambda w, dil=1: pl.BlockSpec((1, tm // dil, dil * w), lambda b, i: (b, i, 0))
    vec = lambda k: pl.BlockSpec((None, None, 1, d), lambda b, i, k=k: (b, k, 0, 0))
    shp = lambda w, dt, dil=1: jax.ShapeDtypeStruct((batch, seq // dil, dil * w), dt)
    dils = [dil for _, dil in DIL_GROUPS]
    return pl.pallas_call(
        _inproj_kernel,
        out_shape=[shp(_QKV_A, BF16, dil) for dil in dils] + [shp(512, BF16)] * 2
        + [shp(128, BF16, CMP_STRIDE)] * 2 + [shp(128, F32)] * 4 + [shp(128, F32)] + [shp(1024, BF16)] * 2,
        grid=(batch, seq // tm),
        in_specs=[
            row(d), vec(0), vec(1),
            pl.BlockSpec((1, d), lambda b, i: (0, 0)),
            pl.BlockSpec((tm, LANES), lambda b, i: (i, 0)),
            pl.BlockSpec((tm, LANES), lambda b, i: (i, 0)),
            _resident((d, N_IN_PACKED)),
        ],
        out_specs=[row(_QKV_A, dil) for dil in dils] + [row(512)] * 2 + [row(128, CMP_STRIDE)] * 2
        + [row(128)] * 5 + [row(1024)] * 2,
        scratch_shapes=[pltpu.VMEM((A_GROUP_WIDTH // LANES, tm, LANES), F32)],
        compiler_params=_params(("arbitrary", "arbitrary")),
        name="input_projection",
    )(x, mod_l, mod_l, norm_g.reshape(1, d), cos_t, sin_t, w_packed)


def _banded_kernel(qkv_ref, o_ref, lse_ref, *, length, nres):
    blk = 128
    nq = max(1, min(BANDED_UNITS // (2 * nres), length // blk))
    nkeys = 2 * blk if length > blk else blk
    lane = _lane_iota((blk, LANES))
    lo = lane < HEAD_DIM
    qrow = _row_iota((2 * blk, nkeys)) % blk
    kcol = _lane_iota((2 * blk, nkeys))

    def rows_of(i, qb):
        r0 = pl.multiple_of((i * nq + qb) * blk, blk)
        ks = pl.multiple_of(jnp.maximum(r0 - blk, 0), blk) if length > blk else 0
        return r0, ks

    def score(i, qb, res, pair):
        base = res * _QKV_A + pair * LANES
        r0, ks = rows_of(i, qb)
        qp = qkv_ref[0, pl.ds(r0, blk), base:base + LANES]
        k2 = qkv_ref[0, pl.ds(ks, nkeys), base + 256:base + 256 + LANES]
        zero = jnp.zeros_like(qp)
        qs = jnp.concatenate([jnp.where(lo, qp, zero), jnp.where(lo, zero, qp)], axis=0)
        s = lax.dot_general(qs, k2, (((1,), (1,)), ((), ())), preferred_element_type=F32)
        dist = (r0 + qrow) - (ks + kcol)
        return jnp.where((dist >= 0) & (dist <= BAND_BACK), s, NEG_INF)

    def soft(s):
        m = jnp.max(s, axis=-1, keepdims=True)
        e = jnp.exp2(s - m)
        den = jnp.sum(e, axis=-1, keepdims=True)
        return e.astype(BF16), 1.0 / den, m * LN2 + jnp.log(den)

    def emit(i, qb, res, pair, e, inv, lse):
        base = res * _QKV_A + pair * LANES
        r0, ks = rows_of(i, qb)
        v2 = qkv_ref[0, pl.ds(ks, nkeys), base + 512:base + 512 + LANES]
        o = jnp.dot(e, v2, preferred_element_type=F32) * inv
        ob = res * A_GROUP_WIDTH + pair * LANES
        o_ref[0, pl.ds(r0, blk), ob:ob + LANES] = jnp.where(lo, o[:blk], o[blk:]).astype(o_ref.dtype)
        lse_ref[0, pl.ds(r0, blk), ob:ob + LANES] = jnp.where(
            lo, jnp.broadcast_to(lse[:blk], (blk, LANES)), jnp.broadcast_to(lse[blk:], (blk, LANES)))

    units = [(qb, res, pair) for qb in range(nq) for res in range(nres) for pair in range(2)]

    def body(i, carry):
        ss = [score(i, *u) for u in units]
        ps = [soft(s) for s in ss]
        for u, p in zip(units, ps):
            emit(i, *u, *p)
        return carry

    lax.fori_loop(0, length // (blk * nq), body, 0)


def _banded_attention(view, dil):
    batch, length, _ = view.shape
    nres = min(dil, 4)
    return pl.pallas_call(
        functools.partial(_banded_kernel, length=length, nres=nres),
        out_shape=[jax.ShapeDtypeStruct((batch, length, dil * A_GROUP_WIDTH), BF16),
                   jax.ShapeDtypeStruct((batch, length, dil * A_GROUP_WIDTH), F32)],
        grid=(batch, dil // nres),
        in_specs=[pl.BlockSpec((1, length, nres * _QKV_A), lambda b, r: (b, 0, r))],
        out_specs=[pl.BlockSpec((1, length, nres * A_GROUP_WIDTH), lambda b, r: (b, 0, r))] * 2,
        compiler_params=_params(("arbitrary", "arbitrary")),
        name=f"banded_attention_d{dil}",
    )(view)


def _compress_kernel(k_ref, v_ref, pe_ref, w1_ref, w2_ref, kc_ref, vc_ref):
    nchunk = k_ref.shape[1]
    for t, (src, dst) in enumerate(((k_ref, kc_ref), (v_ref, vc_ref))):
        c = src[0].astype(F32)
        top = jnp.dot((c + pe_ref[t, 0]).astype(BF16), w1_ref[t, 0], preferred_element_type=F32)
        bot = jnp.dot((c + pe_ref[t, 1]).astype(BF16), w1_ref[t, 1], preferred_element_type=F32)
        pre = top + pltpu.roll(bot, nchunk - 1, 0)
        for g in range(NSA_KV):
            hid = _silu(pre[:, g * CMP_HID:(g + 1) * CMP_HID]).astype(BF16)
            res = jnp.dot(hid, w2_ref[t], preferred_element_type=F32)
            dst[0, g] = (res if t == 0 else res.T).astype(dst.dtype)


def _pack_compress_weights(pe, w1, w2):
    half = CMP_LEN // 2
    w1r = w1.reshape(2, half, HEAD_DIM, CMP_HID)
    zero = jnp.zeros_like(w1r)
    w1p = jnp.stack([jnp.concatenate([w1r, zero], axis=-1), jnp.concatenate([zero, w1r], axis=-1)], axis=2)
    w1p = w1p.reshape(2, half * LANES, NSA_KV * CMP_HID)
    pep = jnp.tile(pe.reshape(2, half, 1, HEAD_DIM), (1, 1, NSA_KV, 1)).reshape(2, 1, half * LANES)
    w2p = jnp.concatenate([w2, w2], axis=1)
    return pep, w1p.astype(BF16), w2p.astype(BF16)


def _compress(kc_raw, vc_raw, pep, w1p, w2p):
    batch, nchunk, width = kc_raw.shape
    view = lambda a: a
    full = lambda a: _resident(a.shape)
    out = jax.ShapeDtypeStruct((batch, NSA_KV, nchunk, LANES), BF16)
    return pl.pallas_call(
        _compress_kernel,
        out_shape=[out, out],
        grid=(batch,),
        in_specs=[pl.BlockSpec((1, nchunk, width), lambda b: (b, 0, 0))] * 2
        + [full(pep), full(w1p), full(w2p)],
        out_specs=[pl.BlockSpec((1, NSA_KV, nchunk, LANES), lambda b: (b, 0, 0, 0))] * 2,
        compiler_params=_params(("arbitrary",)),
        name="nsa_compress",
    )(view(kc_raw), view(vc_raw), pep, w1p, w2p)


def _nsa_kernel(bqr_ref, bqn_ref, blog_ref, kcd_ref, vct_ref, ksl_ref, vsl_ref, kwn_ref, vwn_ref,
                ovl_ref, y_ref, ksel_s, vsel_s, kwin_s, vwin_s, m_s, acc_s, alpha_s, s_s, p_s, *, tile, seq):
    qi = pl.program_id(1)
    q0 = qi * tile
    cols = NSA_REP * tile
    nblk = seq // SEL_LEN
    ncmp = kcd_ref.shape[2]
    ntile = seq // tile
    nt = (((1,), (1,)), ((), ()))

    @pl.when(qi == 0)
    def _build_kv():
        lane = _lane_iota((seq, LANES))
        lo = lane < HEAD_DIM
        blk_of_row = _row_iota((seq, LANES)) // SEL_LEN
        selmask = jnp.where(blk_of_row == lane - HEAD_DIM, -MASK_BIG, 0.0)
        for src, dst, fill in ((ksl_ref, ksel_s, selmask), (kwn_ref, kwin_s, 0.0)):
            a = src[0]
            dst[0] = jnp.where(lo, a, fill).astype(dst.dtype)
            dst[1] = jnp.where(lo, pltpu.roll(a, HEAD_DIM, 1), fill).astype(dst.dtype)
        ones_rows = jnp.where(_row_iota((V_ROWS - HEAD_DIM, seq)) == 0, 1.0, 0.0)
        for src, dst in ((vsl_ref, vsel_s), (vwn_ref, vwin_s)):
            vt = src[0].T
            for g in range(NSA_KV):
                full = jnp.concatenate([vt[g * HEAD_DIM:(g + 1) * HEAD_DIM], ones_rows], axis=0).astype(dst.dtype)
                for kt in range(ntile):
                    dst[g, kt] = full[:, kt * tile:(kt + 1) * tile]

    lo_t = _lane_iota((tile, LANES)) < HEAD_DIM

    def stack_heads(q_ref, g, extra):
        out = []
        for pair in range(2):
            c = g * 256 + pair * LANES
            qp = q_ref[0, :, c:c + LANES].astype(F32)
            out.append(jnp.where(lo_t, qp, extra))
            out.append(jnp.where(lo_t, pltpu.roll(qp, HEAD_DIM, 1), extra))
        return jnp.concatenate(out, axis=0).astype(BF16)

    qpos = q0 + (_lane_iota((tile, cols)) % tile)
    krow = _row_iota((tile, cols))

    def scores(k_s, g, kt, qs, slot):
        k0 = pl.multiple_of(kt * tile, tile)
        s_s[g, slot] = lax.dot_general(k_s[g, pl.ds(k0, tile), :], qs, nt, preferred_element_type=F32)

    def softmax(g, slot, keep):
        s = s_s[g, slot]
        if keep is not None:
            s = jnp.where(keep, s, NEG_INF)
        m_old = m_s[g]
        m_new = jnp.maximum(m_old, jnp.max(s, axis=0, keepdims=True))
        alpha_s[g, slot] = jnp.exp2(m_old - m_new)
        p_s[g, slot] = jnp.exp2(s - m_new).astype(BF16)
        m_s[g] = m_new

    def pv(v_s, g, kt, slot):
        acc_s[g] = alpha_s[g, slot] * acc_s[g] + jnp.dot(v_s[g, kt], p_s[g, slot], preferred_element_type=F32)

    def flash_init(g):
        m_s[g] = jnp.full(m_s.shape[1:], NEG_INF, F32)
        acc_s[g] = jnp.zeros(acc_s.shape[1:], F32)

    def flash_result(g):
        acc = acc_s[g]
        return acc[0:HEAD_DIM] * (1.0 / acc[HEAD_DIM:HEAD_DIM + 1])

    gates = _sigmoid(blog_ref[0].T)
    zero_t = jnp.zeros((tile, LANES), F32)
    blk = _row_iota((nblk, tile))
    tpos = q0 + _lane_iota((nblk, tile))
    groups = range(NSA_KV)
    causal = qpos >= q0 + krow

    def cmp_scores(g):
        qn = stack_heads(bqn_ref, g, zero_t)
        return lax.dot_general(kcd_ref[0, g], qn, nt, preferred_element_type=F32)

    def cmp_probs(s):
        tq = q0 + (_lane_iota((ncmp, cols)) % tile)
        cvis = (_row_iota((ncmp, cols)) * CMP_STRIDE + (CMP_LEN - 1)) <= tq
        s = jnp.where(cvis, s, NEG_INF)
        m = jnp.max(s, axis=0, keepdims=True)
        e = jnp.where(cvis, jnp.exp2(s - m), 0.0)
        den = jnp.sum(e, axis=0, keepdims=True)
        return e * jnp.where(den > 0.0, 1.0 / den, 0.0)

    def cmp_products(g, p):
        o_cmp = jnp.dot(vct_ref[0, g, 0:HEAD_DIM, :], p.astype(BF16), preferred_element_type=F32)
        psum = p[:, 0:tile] + p[:, tile:2 * tile] + p[:, 2 * tile:3 * tile] + p[:, 3 * tile:4 * tile]
        p_hi = psum.astype(BF16)
        p_lo = (psum - p_hi.astype(F32)).astype(BF16)
        ovl = ovl_ref[...]
        imp = (jnp.dot(ovl, p_hi, preferred_element_type=F32)
               + jnp.dot(ovl, p_lo, preferred_element_type=F32))
        return o_cmp, imp

    def select_queries(g, imp):
        cur = tpos // SEL_LEN
        forced = (blk == 0) | (blk == cur) | (blk == cur - 1)
        imp = jnp.where(forced, FORCE_SCORE, imp)
        imp = jnp.where(blk * SEL_LEN <= tpos, imp, NEG_INF)
        rank = jnp.zeros((nblk, tile), F32)
        for k in range(nblk):
            vk = imp[k:k + 1, :]
            tie = jnp.where(blk > k, 1.0, 0.0)
            rank = rank + jnp.where(vk > imp, 1.0, jnp.where(vk == imp, tie, 0.0))
        notsel = jnp.where(rank >= SEL_TOPK, 1.0, 0.0)
        extra = jnp.concatenate([jnp.zeros((HEAD_DIM, tile), F32), notsel,
                                 jnp.zeros((LANES - HEAD_DIM - nblk, tile), F32)], axis=0).T.astype(BF16)
        return q_win[g] + jnp.concatenate([extra] * NSA_REP, axis=0)

    far, near = qi - 2, qi - 1
    keep_far = (qpos - (far * tile + krow)) <= jnp.where(far >= 0, WIN_LEN - 1, -1)
    keep_near = krow >= jnp.where(near >= 0, 0, tile)
    q_win = [stack_heads(bqr_ref, g, zero_t) for g in groups]
    for g in groups:
        flash_init(g)
        scores(kwin_s, g, jnp.maximum(far, 0), q_win[g], 0)
    for g in groups:
        scores(kwin_s, g, jnp.maximum(near, 0), q_win[g], 1)
    cmp_s = [cmp_scores(g) for g in groups]
    for g in groups:
        softmax(g, 0, keep_far)
    cmp_p = [cmp_probs(s) for s in cmp_s]
    for g in groups:
        scores(kwin_s, g, qi, q_win[g], 0)
    for g in groups:
        pv(vwin_s, g, jnp.maximum(far, 0), 0)
    o_cmps, imps = zip(*[cmp_products(g, cmp_p[g]) for g in groups])
    for g in groups:
        softmax(g, 1, keep_near)
    for g in groups:
        pv(vwin_s, g, jnp.maximum(near, 0), 1)
    q_sel = [select_queries(g, imps[g]) for g in groups]
    for g in groups:
        softmax(g, 0, causal)
    for g in groups:
        pv(vwin_s, g, qi, 0)
    o_wins = [flash_result(g) for g in groups]

    odd = qi & 1
    for g in groups:
        flash_init(g)

    @pl.when(odd == 1)
    def _():
        for g in groups:
            scores(ksel_s, g, 0, q_sel[g], 0)
        for g in groups:
            softmax(g, 0, None)
        for g in groups:
            pv(vsel_s, g, 0, 0)

    for g in groups:
        p_s[g, 1] = jnp.zeros(p_s.shape[2:], BF16)
        alpha_s[g, 1] = jnp.ones(alpha_s.shape[2:], F32)
        scores(ksel_s, g, odd, q_sel[g], 0)

    def stage(j, slot):
        for g in groups:
            scores(ksel_s, g, j + 1, q_sel[g], 1 - slot)
        for g in groups:
            pv(vsel_s, g, jnp.maximum(j - 1, 0), 1 - slot)
        for g in groups:
            softmax(g, slot, None)

    def sel_body(jj, carry):
        j = odd + 2 * jj
        stage(j, 0)
        stage(j + 1, 1)
        return carry

    lax.fori_loop(0, (qi - odd) // 2, sel_body, 0)
    for g in groups:
        pv(vsel_s, g, jnp.maximum(qi - 1, 0), 1)
    for g in groups:
        softmax(g, 0, causal)
    for g in groups:
        pv(vsel_s, g, qi, 0)
    o_sels = [flash_result(g) for g in groups]

    heads = []
    for g in groups:
        for r in range(NSA_REP):
            c = (g * NSA_REP + r) * 3
            sl = slice(r * tile, (r + 1) * tile)
            heads.append(gates[c:c + 1] * o_cmps[g][:, sl] + gates[c + 1:c + 2] * o_sels[g][:, sl]
                         + gates[c + 2:c + 3] * o_wins[g][:, sl])
    y_ref[0] = jnp.concatenate(heads, axis=0).T.astype(y_ref.dtype)


def _overlap_matrix(seq):
    starts = np.arange(seq // CMP_STRIDE) * CMP_STRIDE
    bstart = np.arange(seq // SEL_LEN) * SEL_LEN
    ovl = (starts[None, :] < bstart[:, None] + SEL_LEN) & (starts[None, :] + CMP_LEN > bstart[:, None])
    return jnp.asarray(ovl, dtype=BF16)


def _nsa_attention(bqr, bqn, blog, kcd, vct, ksl, vsl, kwn, vwn):
    batch, seq, _ = bqr.shape
    tile = NSA_TILE
    assert 2 * tile == WIN_LEN
    cols = NSA_REP * tile
    ncmp = seq // CMP_STRIDE
    nblk = seq // SEL_LEN
    qspec = lambda w: pl.BlockSpec((1, tile, w), lambda b, i: (b, i, 0))
    kvspec = pl.BlockSpec((1, seq, LANES), lambda b, i: (b, 0, 0))
    cspec = pl.BlockSpec((1, NSA_KV, ncmp, LANES), lambda b, i: (b, 0, 0, 0))
    k_scratch = pltpu.VMEM((NSA_KV, seq, LANES), BF16)
    v_scratch = pltpu.VMEM((NSA_KV, seq // tile, V_ROWS, tile), BF16)
    return pl.pallas_call(
        functools.partial(_nsa_kernel, tile=tile, seq=seq),
        out_shape=jax.ShapeDtypeStruct((batch, seq, NSA_Q_WIDTH), BF16),
        grid=(batch, seq // tile),
        in_specs=[qspec(NSA_Q_WIDTH), qspec(NSA_Q_WIDTH), qspec(LANES), cspec, cspec,
                  kvspec, kvspec, kvspec, kvspec,
                  pl.BlockSpec((nblk, ncmp), lambda b, i: (0, 0))],
        out_specs=qspec(NSA_Q_WIDTH),
        scratch_shapes=[k_scratch, v_scratch, k_scratch, v_scratch,
                        pltpu.VMEM((NSA_KV, 1, cols), F32), pltpu.VMEM((NSA_KV, V_ROWS, cols), F32),
                        pltpu.VMEM((NSA_KV, 2, 1, cols), F32), pltpu.VMEM((NSA_KV, 2, tile, cols), F32),
                        pltpu.VMEM((NSA_KV, 2, tile, cols), BF16)],
        compiler_params=_params(("arbitrary", "arbitrary")),
        name="nsa_attention",
    )(bqr, bqn, blog, kcd, vct, ksl, vsl, kwn, vwn, _overlap_matrix(seq))


def _merge_ffn_kernel(o0_ref, o1_ref, o2_ref, l0_ref, l1_ref, l2_ref, yb_ref, ga_ref, gb_ref, x_ref,
                      gate_ref, wa_ref, wb_ref, wo_ref, shift2_ref, scale2_ref, gate2_ref, g2_ref,
                      wup_ref, cw_ref, cb_ref, wdn_ref, fin_ref, out_ref, il_s, he_s, act_s, carry_s,
                      *, final):
    i = pl.program_id(1)
    tm = x_ref.shape[1]

    def natural(ref, dil, slot):
        if dil == 1:
            return ref[0].astype(F32)
        nblk = A_GROUP_WIDTH // LANES
        for r in range(dil):
            for j in range(nblk):
                c = r * A_GROUP_WIDTH + j * LANES
                il_s[slot, j, pl.ds(r, tm // dil, stride=dil), :] = ref[0, :, c:c + LANES].astype(F32)
        return jnp.concatenate([il_s[slot, j] for j in range(nblk)], axis=1)

    dils = [dil for _, dil in DIL_GROUPS]
    o0, o1, o2 = (natural(r, dil, n) for n, (r, dil) in enumerate(zip((o0_ref, o1_ref, o2_ref), dils)))
    l0, l1, l2 = (natural(r, dil, 3 + n) for n, (r, dil) in enumerate(zip((l0_ref, l1_ref, l2_ref), dils)))
    m = jnp.maximum(jnp.maximum(l0, l1), l2)
    e0, e1, e2 = jnp.exp(l0 - m), jnp.exp(l1 - m), jnp.exp(l2 - m)
    inv = 1.0 / (e0 + e1 + e2)
    y_a = (e0 * inv) * o0 + (e1 * inv) * o1 + (e2 * inv) * o2
    pa = jnp.dot(y_a.astype(BF16), wa_ref[...], preferred_element_type=F32)
    pb = jnp.dot(yb_ref[0], wb_ref[...], preferred_element_type=F32)
    merged = _sigmoid(ga_ref[0].astype(F32)) * pa + _sigmoid(gb_ref[0].astype(F32)) * pb
    out = jnp.dot(merged.astype(BF16), wo_ref[...], preferred_element_type=F32)
    out_ref[0] = x_ref[0] + gate_ref[...] * out

    @pl.when(i == 0)
    def _():
        carry_s[...] = jnp.zeros(carry_s.shape, F32)

    hx = _norm_mod(out_ref[0], g2_ref[...], scale2_ref[...], shift2_ref[...])
    he_s[0:tm] = hx.astype(BF16)
    he_s[tm:] = carry_s[...].astype(BF16)
    carry_s[...] = hx[tm - HALO:]

    def up(col):
        return jnp.dot(he_s[...], wup_ref[:, col:col + FFN_CHUNK], preferred_element_type=F32)

    def conv(u, col):
        w = cw_ref[:, col:col + FFN_CHUNK]
        return (w[0:1] * pltpu.roll(u, 2, 0)[:tm] + w[1:2] * pltpu.roll(u, 1, 0)[:tm]
                + (w[2:3] * u[:tm] + cb_ref[:, col:col + FFN_CHUNK]))

    chunks = list(range(0, D_FF, FFN_CHUNK))
    pending = [(up(c), up(D_FF + c)) for c in chunks[:FFN_LOOKAHEAD]]
    acc = None
    lo_c = 0
    group = -(-len(chunks) // FFN_DOWN_SPLITS)
    for n, c in enumerate(chunks):
        ug, uv = pending.pop(0)
        if n + FFN_LOOKAHEAD < len(chunks):
            ahead = chunks[n + FFN_LOOKAHEAD]
            pending.append((up(ahead), up(D_FF + ahead)))
        act_s[:, c:c + FFN_CHUNK] = (_silu(conv(ug, c)) * conv(uv, D_FF + c)).astype(BF16)
        if (n + 1) % group == 0 or n + 1 == len(chunks):
            hi_c = c + FFN_CHUNK
            part = jnp.dot(act_s[:, lo_c:hi_c], wdn_ref[lo_c:hi_c, :], preferred_element_type=F32)
            acc = part if acc is None else acc + part
            lo_c = hi_c
    y = out_ref[0] + gate2_ref[...] * acc
    if final:
        ms = jnp.mean(y * y, axis=-1, keepdims=True)
        y = y * lax.rsqrt(ms + RMS_EPS) * fin_ref[...]
    out_ref[0] = y


def _merge_ffn(o_lse, y_b, ga, gb, x, mod_l, w_a, w_b, w_o, norm_g, w_up, conv_w, conv_b, w_down,
               final_g, final):
    batch, seq, d = x.shape
    tm = TAIL_TILE
    row = lambda w, dil=1: pl.BlockSpec((1, tm // dil, dil * w), lambda b, i: (b, i, 0))
    vec = lambda k: pl.BlockSpec((None, None, 1, d), lambda b, i, k=k: (b, k, 0, 0))
    full = lambda a: _resident(a.shape)
    (o0, l0), (o1, l1), (o2, l2) = o_lse
    grp = [row(A_GROUP_WIDTH, dil) for _, dil in DIL_GROUPS]
    norm_g = norm_g.reshape(1, d)
    conv_b = conv_b.reshape(1, -1)
    final_g = final_g.reshape(1, d)
    return pl.pallas_call(
        functools.partial(_merge_ffn_kernel, final=final),
        out_shape=jax.ShapeDtypeStruct(x.shape, F32),
        grid=(batch, seq // tm),
        in_specs=grp + grp + [row(NSA_Q_WIDTH), row(d), row(d), row(d), vec(2),
                              full(w_a), full(w_b), full(w_o), vec(3), vec(4), vec(5), full(norm_g),
                              full(w_up), full(conv_w), full(conv_b), full(w_down), full(final_g)],
        out_specs=row(d),
        scratch_shapes=[pltpu.VMEM((6, A_GROUP_WIDTH // LANES, tm, LANES), F32),
                        pltpu.VMEM((tm + HALO, d), BF16), pltpu.VMEM((tm, D_FF), BF16),
                        pltpu.VMEM((HALO, d), F32)],
        compiler_params=_params(("arbitrary", "arbitrary")),
        name="merge_ffn",
    )(o0, o1, o2, l0, l1, l2, y_b, ga, gb, x, mod_l, w_a, w_b, w_o, mod_l, mod_l, mod_l, norm_g,
      w_up, conv_w, conv_b, w_down, final_g)


def _rope_tables(seq):
    inv = 1.0 / (ROPE_THETA ** (jnp.arange(0, HEAD_DIM, 2, dtype=F32) / HEAD_DIM))
    ang = jnp.arange(seq, dtype=F32)[:, None] * inv[None, :]
    cos, sin = jnp.cos(ang), jnp.sin(ang)
    cos_t = jnp.tile(cos, (1, LANES // (HEAD_DIM // 2)))
    sin_t = jnp.tile(jnp.concatenate([-sin, sin], axis=1), (1, LANES // HEAD_DIM))
    return cos_t, sin_t


def kernel(x, c, norm1_g, norm2_g, final_g, w_mod, b_mod, w_in, cmp_pe_k, cmp_pe_v, cmp_w1_k, cmp_w2_k, cmp_w1_v, cmp_w2_v, w_br_a, w_br_b, w_out, w_up, conv_w, conv_b, w_down):
    batch, seq, d = x.shape
    depth = w_in.shape[0]
    cos_t, sin_t = _rope_tables(seq)
    mod = _modulation(c, w_mod, b_mod).reshape(depth, batch, N_MOD, 1, d)
    for layer in range(depth):
        mod_l = mod[layer]
        (qkv0, qkv1, qkv2, bqr, bqn, kc_raw, vc_raw, ksl, vsl, kwn, vwn, blog, ga, gb) = _input_projection(
            x, mod_l, norm1_g[layer], cos_t, sin_t, _pack_w_in(w_in[layer]))
        o_lse = [_banded_attention(qkv, dil) for qkv, (_, dil) in zip((qkv0, qkv1, qkv2), DIL_GROUPS)]

        pk = _pack_compress_weights(cmp_pe_k[layer], cmp_w1_k[layer], cmp_w2_k[layer])
        pv = _pack_compress_weights(cmp_pe_v[layer], cmp_w1_v[layer], cmp_w2_v[layer])
        pep, w1p, w2p = (jnp.stack([a, b]) for a, b in zip(pk, pv))
        kcd, vcd = _compress(kc_raw, vc_raw, pep, w1p, w2p)
        y_b = _nsa_attention(bqr, bqn, blog, kcd, vcd, ksl, vsl, kwn, vwn)

        x = _merge_ffn(o_lse, y_b, ga, gb, x, mod_l, w_br_a[layer].astype(BF16),
                       w_br_b[layer].astype(BF16), w_out[layer].astype(BF16), norm2_g[layer],
                       w_up[layer].astype(BF16), conv_w[layer], conv_b[layer],
                       w_down[layer].astype(BF16), final_g, final=(layer == depth - 1))
    return x
```

```python
import functools

import jax
import jax.numpy as jnp
import numpy as np
from jax import lax
from jax.experimental import pallas as pl
from jax.experimental.pallas import tpu as pltpu

F32 = jnp.float32
BF16 = jnp.bfloat16

HEAD_DIM = 64
ROPE_THETA = 10000.0
RMS_EPS = 1e-6
NEG_INF = -1e30
FORCE_SCORE = 1e9
MASK_BIG = float(2.0 ** 100)
LOG2E = 1.4426950408889634
LN2 = 0.6931471805599453

DIL_GROUPS = ((128, 1), (512, 4), (2048, 16))
A_HEADS = 4
A_GROUP_WIDTH = A_HEADS * HEAD_DIM
BAND_BACK = 128
NSA_HEADS = 8
NSA_KV = 2
NSA_REP = NSA_HEADS // NSA_KV
NSA_Q_WIDTH = NSA_HEADS * HEAD_DIM
CMP_LEN = 32
CMP_STRIDE = 16
CMP_HID = 256
SEL_LEN = 64
SEL_TOPK = 16
WIN_LEN = 512
D_FF = 2816
CONV_W = 3
N_MOD = 6

LANES = 128
VMEM_LIMIT = 56 * 1024 * 1024

ROW_TILE = 1024
TAIL_TILE = 512
NSA_TILE = 256
BANDED_UNITS = 8
FFN_CHUNK = 256
FFN_LOOKAHEAD = 2
FFN_DOWN_SPLITS = 1
HALO = 8
V_ROWS = 80


def _sigmoid(x):
    return 0.5 * jnp.tanh(0.5 * x) + 0.5


def _silu(x):
    h = 0.5 * x
    return h + h * jnp.tanh(h)


def _params(semantics):
    return pltpu.CompilerParams(dimension_semantics=semantics, vmem_limit_bytes=VMEM_LIMIT)


def _resident(shape):
    nd = len(shape)
    return pl.BlockSpec(tuple(shape), lambda *_: (0,) * nd, pipeline_mode=pl.Buffered(1))


def _lane_iota(shape):
    return lax.broadcasted_iota(jnp.int32, shape, len(shape) - 1)


def _row_iota(shape):
    return lax.broadcasted_iota(jnp.int32, shape, len(shape) - 2)


def _mod_kernel(c_ref, w_ref, b_ref, o_ref):
    k = pl.program_id(1)
    part = jnp.dot(_silu(c_ref[...]), w_ref[0], preferred_element_type=F32)

    @pl.when(k == 0)
    def _():
        o_ref[0] = part + b_ref[0]

    @pl.when(k > 0)
    def _():
        o_ref[0] += part


def _modulation(c, w_mod, b_mod):
    depth, d, n = w_mod.shape
    batch = c.shape[0]
    tk = 256
    return pl.pallas_call(
        _mod_kernel,
        out_shape=jax.ShapeDtypeStruct((depth, batch, n), F32),
        grid=(depth, d // tk),
        in_specs=[
            pl.BlockSpec((batch, tk), lambda l, k: (0, k)),
            pl.BlockSpec((1, tk, n), lambda l, k: (l, k, 0)),
            pl.BlockSpec((1, 1, n), lambda l, k: (l, 0, 0)),
        ],
        out_specs=pl.BlockSpec((1, batch, n), lambda l, k: (l, 0, 0)),
        compiler_params=_params(("arbitrary", "arbitrary")),
        name="modulation",
    )(c, w_mod, b_mod.reshape(depth, 1, n))


def _norm_mod(x, g, scale, shift):
    ms = jnp.mean(x * x, axis=-1, keepdims=True)
    y = x * lax.rsqrt(ms + RMS_EPS) * g
    return y * (1.0 + scale) + shift


def _rope_chunk(xc, cos, sin_signed, first_half):
    partner = jnp.where(first_half, pltpu.roll(xc, LANES - HEAD_DIM // 2, 1),
                        pltpu.roll(xc, HEAD_DIM // 2, 1))
    return xc * cos + partner * sin_signed


_QKV_A = 3 * A_GROUP_WIDTH
_A_WIDTH = 3 * A_GROUP_WIDTH
_OFF_BQ = 3 * _A_WIDTH
_OFF_KV = _OFF_BQ + NSA_Q_WIDTH
_OFF_TAIL = _OFF_KV + 6 * LANES
_N_BLOG = NSA_HEADS * 3
_TAIL_WIDTH = 17 * LANES
N_IN_PACKED = _OFF_TAIL + _TAIL_WIDTH


def _inproj_kernel(x_ref, shift_ref, scale_ref, g_ref, cos_ref, sin_ref, w_ref,
                   qkv0_ref, qkv1_ref, qkv2_ref, bqr_ref, bqn_ref, kc_ref, vc_ref,
                   ksl_ref, vsl_ref, kwn_ref, vwn_ref, blog_ref, ga_ref, gb_ref, il_s):
    h = _norm_mod(x_ref[0], g_ref[...], scale_ref[...], shift_ref[...]).astype(BF16)
    cos = cos_ref[...]
    sin = sin_ref[...]
    rows = cos.shape[0]
    first_half = (_lane_iota((rows, LANES)) % HEAD_DIM) < (HEAD_DIM // 2)
    scale = HEAD_DIM ** -0.5 * LOG2E

    def proj(off, width):
        return jnp.dot(h, w_ref[:, off:off + width], preferred_element_type=F32)

    def rope(y, mult):
        parts = [_rope_chunk(y[:, c:c + LANES], cos, sin, first_half) * mult
                 for c in range(0, y.shape[1], LANES)]
        return parts[0] if len(parts) == 1 else jnp.concatenate(parts, axis=1)

    def store_strided(out_ref, y, dil, col):
        nblk = y.shape[1] // LANES
        width = out_ref.shape[2] // dil
        for j in range(nblk):
            il_s[j] = y[:, j * LANES:(j + 1) * LANES]
        for r in range(dil):
            for j in range(nblk):
                c = r * width + col + j * LANES
                out_ref[0, :, c:c + LANES] = il_s[j, pl.ds(r, rows // dil, stride=dil), :].astype(out_ref.dtype)

    for g, ((_, dil), out_ref) in enumerate(zip(DIL_GROUPS, (qkv0_ref, qkv1_ref, qkv2_ref))):
        base = g * A_GROUP_WIDTH
        pieces = (rope(proj(base, 256), scale), rope(proj(_A_WIDTH + base, 256), 1.0),
                  proj(2 * _A_WIDTH + base, 256))
        for n, y in enumerate(pieces):
            if dil == 1:
                out_ref[0, :, n * 256:(n + 1) * 256] = y.astype(out_ref.dtype)
            else:
                store_strided(out_ref, y, dil, n * 256)

    bq = proj(_OFF_BQ, NSA_Q_WIDTH)
    bqn_ref[0] = (bq * scale).astype(bqn_ref.dtype)
    bqr_ref[0] = rope(bq, scale).astype(bqr_ref.dtype)

    kv = proj(_OFF_KV, 6 * LANES)
    store_strided(kc_ref, kv[:, 0:128], CMP_STRIDE, 0)
    store_strided(vc_ref, kv[:, 128:256], CMP_STRIDE, 0)
    ksl_ref[0] = rope(kv[:, 256:384], 1.0).astype(ksl_ref.dtype)
    vsl_ref[0] = kv[:, 384:512].astype(vsl_ref.dtype)
    kwn_ref[0] = rope(kv[:, 512:640], 1.0).astype(kwn_ref.dtype)
    vwn_ref[0] = kv[:, 640:768].astype(vwn_ref.dtype)

    tail = proj(_OFF_TAIL, _TAIL_WIDTH)
    blog_ref[0] = tail[:, 0:LANES]
    ga_ref[0] = tail[:, _N_BLOG:_N_BLOG + 1024].astype(ga_ref.dtype)
    gb_ref[0] = tail[:, _N_BLOG + 1024:_N_BLOG + 2048].astype(gb_ref.dtype)


def _cast_kernel(*refs):
    *in_refs, o_ref = refs
    col = 0
    for r in in_refs:
        o_ref[:, col:col + r.shape[1]] = r[...].astype(o_ref.dtype)
        col += r.shape[1]


def _layer_weight_bf16(w, layer, tail=None):
    _, rows, cols = w.shape
    tr = min(rows, 256)
    main = cols if tail is None else cols // LANES * LANES
    ins = [w] if tail is None else [w, tail]
    widths = [main] if tail is None else [main, tail.shape[2]]
    return pl.pallas_call(
        _cast_kernel,
        out_shape=jax.ShapeDtypeStruct((rows, sum(widths)), BF16),
        grid=(rows // tr,),
        in_specs=[pl.BlockSpec((None, tr, wd), lambda i: (layer, i, 0)) for wd in widths],
        out_specs=pl.BlockSpec((tr, sum(widths)), lambda i: (i, 0)),
        compiler_params=_params(("arbitrary",)),
        name="weight_to_bf16",
    )(*ins)


def _input_projection(x, mod_l, norm_g, cos_t, sin_t, w_packed):
    batch, seq, d = x.shape
    tm = ROW_TILE
    row = lambda w, dil=1: pl.BlockSpec((1, tm // dil, dil * w), lambda b, i: (b, i, 0))
    vec = lambda k: pl.BlockSpec((None, None, 1, d), lambda b, i, k=k: (b, k, 0, 0))
    shp = lambda w, dt, dil=1: jax.ShapeDtypeStruct((batch, seq // dil, dil * w), dt)
    dils = [dil for _, dil in DIL_GROUPS]
    return pl.pallas_call(
        _inproj_kernel,
        out_shape=[shp(_QKV_A, BF16, dil) for dil in dils] + [shp(512, BF16)] * 2
        + [shp(128, BF16, CMP_STRIDE)] * 2 + [shp(128, F32)] * 4 + [shp(128, F32)] + [shp(1024, BF16)] * 2,
        grid=(batch, seq // tm),
        in_specs=[
            row(d), vec(0), vec(1),
            pl.BlockSpec((1, d), lambda b, i: (0, 0)),
            pl.BlockSpec((tm, LANES), lambda b, i: (i, 0)),
            pl.BlockSpec((tm, LANES), lambda b, i: (i, 0)),
            _resident((d, N_IN_PACKED)),
        ],
        out_specs=[row(_QKV_A, dil) for dil in dils] + [row(512)] * 2 + [row(128, CMP_STRIDE)] * 2
        + [row(128)] * 5 + [row(1024)] * 2,
        scratch_shapes=[pltpu.VMEM((A_GROUP_WIDTH // LANES, tm, LANES), F32)],
        compiler_params=_params(("arbitrary", "arbitrary")),
        name="input_projection",
    )(x, mod_l, mod_l, norm_g.reshape(1, d), cos_t, sin_t, w_packed)


def _banded_kernel(qkv_ref, o_ref, lse_ref, *, length, nres):
    blk = 128
    nq = max(1, min(BANDED_UNITS // (2 * nres), length // blk))
    nkeys = 2 * blk if length > blk else blk
    lane = _lane_iota((blk, LANES))
    lo = lane < HEAD_DIM
    qrow = _row_iota((2 * blk, nkeys)) % blk
    kcol = _lane_iota((2 * blk, nkeys))

    def rows_of(i, qb):
        r0 = pl.multiple_of((i * nq + qb) * blk, blk)
        ks = pl.multiple_of(jnp.maximum(r0 - blk, 0), blk) if length > blk else 0
        return r0, ks

    def score(i, qb, res, pair):
        base = res * _QKV_A + pair * LANES
        r0, ks = rows_of(i, qb)
        qp = qkv_ref[0, pl.ds(r0, blk), base:base + LANES]
        k2 = qkv_ref[0, pl.ds(ks, nkeys), base + 256:base + 256 + LANES]
        zero = jnp.zeros_like(qp)
        qs = jnp.concatenate([jnp.where(lo, qp, zero), jnp.where(lo, zero, qp)], axis=0)
        s = lax.dot_general(qs, k2, (((1,), (1,)), ((), ())), preferred_element_type=F32)
        dist = (r0 + qrow) - (ks + kcol)
        return jnp.where((dist >= 0) & (dist <= BAND_BACK), s, NEG_INF)

    def soft(s):
        m = jnp.max(s, axis=-1, keepdims=True)
        e = jnp.exp2(s - m)
        den = jnp.sum(e, axis=-1, keepdims=True)
        return e.astype(BF16), 1.0 / den, m * LN2 + jnp.log(den)

    def emit(i, qb, res, pair, e, inv, lse):
        base = res * _QKV_A + pair * LANES
        r0, ks = rows_of(i, qb)
        v2 = qkv_ref[0, pl.ds(ks, nkeys), base + 512:base + 512 + LANES]
        o = jnp.dot(e, v2, preferred_element_type=F32) * inv
        ob = res * A_GROUP_WIDTH + pair * LANES
        o_ref[0, pl.ds(r0, blk), ob:ob + LANES] = jnp.where(lo, o[:blk], o[blk:]).astype(o_ref.dtype)
        lse_ref[0, pl.ds(r0, blk), ob:ob + LANES] = jnp.where(
            lo, jnp.broadcast_to(lse[:blk], (blk, LANES)), jnp.broadcast_to(lse[blk:], (blk, LANES)))

    units = [(qb, res, pair) for qb in range(nq) for res in range(nres) for pair in range(2)]

    def body(i, carry):
        ss = [score(i, *u) for u in units]
        ps = [soft(s) for s in ss]
        for u, p in zip(units, ps):
            emit(i, *u, *p)
        return carry

    lax.fori_loop(0, length // (blk * nq), body, 0)


def _banded_attention(view, dil):
    batch, length, _ = view.shape
    nres = min(dil, 4)
    return pl.pallas_call(
        functools.partial(_banded_kernel, length=length, nres=nres),
        out_shape=[jax.ShapeDtypeStruct((batch, length, dil * A_GROUP_WIDTH), BF16),
                   jax.ShapeDtypeStruct((batch, length, dil * A_GROUP_WIDTH), F32)],
        grid=(batch, dil // nres),
        in_specs=[pl.BlockSpec((1, length, nres * _QKV_A), lambda b, r: (b, 0, r))],
        out_specs=[pl.BlockSpec((1, length, nres * A_GROUP_WIDTH), lambda b, r: (b, 0, r))] * 2,
        compiler_params=_params(("arbitrary", "arbitrary")),
        name=f"banded_attention_d{dil}",
    )(view)


def _compress_kernel(k_ref, v_ref, pe_ref, w1_ref, w2_ref, kc_ref, vc_ref):
    nchunk = k_ref.shape[1]
    for t, (src, dst) in enumerate(((k_ref, kc_ref), (v_ref, vc_ref))):
        c = src[0].astype(F32)
        top = jnp.dot((c + pe_ref[t, 0]).astype(BF16), w1_ref[t, 0], preferred_element_type=F32)
        bot = jnp.dot((c + pe_ref[t, 1]).astype(BF16), w1_ref[t, 1], preferred_element_type=F32)
        pre = top + pltpu.roll(bot, nchunk - 1, 0)
        for g in range(NSA_KV):
            hid = _silu(pre[:, g * CMP_HID:(g + 1) * CMP_HID]).astype(BF16)
            res = jnp.dot(hid, w2_ref[t], preferred_element_type=F32)
            dst[0, g] = (res if t == 0 else res.T).astype(dst.dtype)


def _pack_compress_weights(pe, w1, w2):
    half = CMP_LEN // 2
    w1r = w1.reshape(2, half, HEAD_DIM, CMP_HID)
    zero = jnp.zeros_like(w1r)
    w1p = jnp.stack([jnp.concatenate([w1r, zero], axis=-1), jnp.concatenate([zero, w1r], axis=-1)], axis=2)
    w1p = w1p.reshape(2, half * LANES, NSA_KV * CMP_HID)
    pep = jnp.tile(pe.reshape(2, half, 1, HEAD_DIM), (1, 1, NSA_KV, 1)).reshape(2, 1, half * LANES)
    w2p = jnp.concatenate([w2, w2], axis=1)
    return pep, w1p.astype(BF16), w2p.astype(BF16)


def _compress(kc_raw, vc_raw, pep, w1p, w2p):
    batch, nchunk, width = kc_raw.shape
    view = lambda a: a
    full = lambda a: _resident(a.shape)
    out = jax.ShapeDtypeStruct((batch, NSA_KV, nchunk, LANES), BF16)
    return pl.pallas_call(
        _compress_kernel,
        out_shape=[out, out],
        grid=(batch,),
        in_specs=[pl.BlockSpec((1, nchunk, width), lambda b: (b, 0, 0))] * 2
        + [full(pep), full(w1p), full(w2p)],
        out_specs=[pl.BlockSpec((1, NSA_KV, nchunk, LANES), lambda b: (b, 0, 0, 0))] * 2,
        compiler_params=_params(("arbitrary",)),
        name="nsa_compress",
    )(view(kc_raw), view(vc_raw), pep, w1p, w2p)


def _nsa_kernel(bqr_ref, bqn_ref, blog_ref, kcd_ref, vct_ref, ksl_ref, vsl_ref, kwn_ref, vwn_ref,
                ovl_ref, y_ref, ksel_s, vsel_s, kwin_s, vwin_s, m_s, acc_s, alpha_s, s_s, p_s, *, tile, seq):
    qi = pl.program_id(1)
    q0 = qi * tile
    cols = NSA_REP * tile
    nblk = seq // SEL_LEN
    ncmp = kcd_ref.shape[2]
    ntile = seq // tile
    nt = (((1,), (1,)), ((), ()))

    @pl.when(qi == 0)
    def _build_kv():
        lane = _lane_iota((seq, LANES))
        lo = lane < HEAD_DIM
        blk_of_row = _row_iota((seq, LANES)) // SEL_LEN
        selmask = jnp.where(blk_of_row == lane - HEAD_DIM, -MASK_BIG, 0.0)
        for src, dst, fill in ((ksl_ref, ksel_s, selmask), (kwn_ref, kwin_s, 0.0)):
            a = src[0]
            dst[0] = jnp.where(lo, a, fill).astype(dst.dtype)
            dst[1] = jnp.where(lo, pltpu.roll(a, HEAD_DIM, 1), fill).astype(dst.dtype)
        ones_rows = jnp.where(_row_iota((V_ROWS - HEAD_DIM, seq)) == 0, 1.0, 0.0)
        for src, dst in ((vsl_ref, vsel_s), (vwn_ref, vwin_s)):
            vt = src[0].T
            for g in range(NSA_KV):
                full = jnp.concatenate([vt[g * HEAD_DIM:(g + 1) * HEAD_DIM], ones_rows], axis=0).astype(dst.dtype)
                for kt in range(ntile):
                    dst[g, kt] = full[:, kt * tile:(kt + 1) * tile]

    lo_t = _lane_iota((tile, LANES)) < HEAD_DIM

    def stack_heads(q_ref, g, extra):
        out = []
        for pair in range(2):
            c = g * 256 + pair * LANES
            qp = q_ref[0, :, c:c + LANES].astype(F32)
            out.append(jnp.where(lo_t, qp, extra))
            out.append(jnp.where(lo_t, pltpu.roll(qp, HEAD_DIM, 1), extra))
        return jnp.concatenate(out, axis=0).astype(BF16)

    qpos = q0 + (_lane_iota((tile, cols)) % tile)
    krow = _row_iota((tile, cols))

    def scores(k_s, g, kt, qs, slot):
        k0 = pl.multiple_of(kt * tile, tile)
        s_s[g, slot] = lax.dot_general(k_s[g, pl.ds(k0, tile), :], qs, nt, preferred_element_type=F32)

    def softmax(g, slot, keep):
        s = s_s[g, slot]
        if keep is not None:
            s = jnp.where(keep, s, NEG_INF)
        m_old = m_s[g]
        m_new = jnp.maximum(m_old, jnp.max(s, axis=0, keepdims=True))
        alpha_s[g, slot] = jnp.exp2(m_old - m_new)
        p_s[g, slot] = jnp.exp2(s - m_new).astype(BF16)
        m_s[g] = m_new

    def pv(v_s, g, kt, slot):
        acc_s[g] = alpha_s[g, slot] * acc_s[g] + jnp.dot(v_s[g, kt], p_s[g, slot], preferred_element_type=F32)

    def flash_init(g):
        m_s[g] = jnp.full(m_s.shape[1:], NEG_INF, F32)
        acc_s[g] = jnp.zeros(acc_s.shape[1:], F32)

    def flash_result(g):
        acc = acc_s[g]
        return acc[0:HEAD_DIM] * (1.0 / acc[HEAD_DIM:HEAD_DIM + 1])

    gates = _sigmoid(blog_ref[0].T)
    zero_t = jnp.zeros((tile, LANES), F32)
    blk = _row_iota((nblk, tile))
    tpos = q0 + _lane_iota((nblk, tile))
    groups = range(NSA_KV)
    causal = qpos >= q0 + krow

    def cmp_scores(g):
        qn = stack_heads(bqn_ref, g, zero_t)
        return lax.dot_general(kcd_ref[0, g], qn, nt, preferred_element_type=F32)

    def cmp_probs(s):
        tq = q0 + (_lane_iota((ncmp, cols)) % tile)
        cvis = (_row_iota((ncmp, cols)) * CMP_STRIDE + (CMP_LEN - 1)) <= tq
        s = jnp.where(cvis, s, NEG_INF)
        m = jnp.max(s, axis=0, keepdims=True)
        e = jnp.where(cvis, jnp.exp2(s - m), 0.0)
        den = jnp.sum(e, axis=0, keepdims=True)
        return e * jnp.where(den > 0.0, 1.0 / den, 0.0)

    def cmp_products(g, p):
        o_cmp = jnp.dot(vct_ref[0, g, 0:HEAD_DIM, :], p.astype(BF16), preferred_element_type=F32)
        psum = p[:, 0:tile] + p[:, tile:2 * tile] + p[:, 2 * tile:3 * tile] + p[:, 3 * tile:4 * tile]
        p_hi = psum.astype(BF16)
        p_lo = (psum - p_hi.astype(F32)).astype(BF16)
        ovl = ovl_ref[...]
        imp = (jnp.dot(ovl, p_hi, preferred_element_type=F32)
               + jnp.dot(ovl, p_lo, preferred_element_type=F32))
        return o_cmp, imp

    def select_queries(g, imp):
        cur = tpos // SEL_LEN
        forced = (blk == 0) | (blk == cur) | (blk == cur - 1)
        imp = jnp.where(forced, FORCE_SCORE, imp)
        imp = jnp.where(blk * SEL_LEN <= tpos, imp, NEG_INF)
        rank = jnp.zeros((nblk, tile), F32)
        for k in range(nblk):
            vk = imp[k:k + 1, :]
            tie = jnp.where(blk > k, 1.0, 0.0)
            rank = rank + jnp.where(vk > imp, 1.0, jnp.where(vk == imp, tie, 0.0))
        notsel = jnp.where(rank >= SEL_TOPK, 1.0, 0.0)
        extra = jnp.concatenate([jnp.zeros((HEAD_DIM, tile), F32), notsel,
                                 jnp.zeros((LANES - HEAD_DIM - nblk, tile), F32)], axis=0).T.astype(BF16)
        return q_win[g] + jnp.concatenate([extra] * NSA_REP, axis=0)

    far, near = qi - 2, qi - 1
    keep_far = (qpos - (far * tile + krow)) <= jnp.where(far >= 0, WIN_LEN - 1, -1)
    keep_near = krow >= jnp.where(near >= 0, 0, tile)
    q_win = [stack_heads(bqr_ref, g, zero_t) for g in groups]
    for g in groups:
        flash_init(g)
        scores(kwin_s, g, jnp.maximum(far, 0), q_win[g], 0)
    for g in groups:
        scores(kwin_s, g, jnp.maximum(near, 0), q_win[g], 1)
    cmp_s = [cmp_scores(g) for g in groups]
    for g in groups:
        softmax(g, 0, keep_far)
    cmp_p = [cmp_probs(s) for s in cmp_s]
    for g in groups:
        scores(kwin_s, g, qi, q_win[g], 0)
    for g in groups:
        pv(vwin_s, g, jnp.maximum(far, 0), 0)
    o_cmps, imps = zip(*[cmp_products(g, cmp_p[g]) for g in groups])
    for g in groups:
        softmax(g, 1, keep_near)
    for g in groups:
        pv(vwin_s, g, jnp.maximum(near, 0), 1)
    q_sel = [select_queries(g, imps[g]) for g in groups]
    for g in groups:
        softmax(g, 0, causal)
    for g in groups:
        pv(vwin_s, g, qi, 0)
    o_wins = [flash_result(g) for g in groups]

    odd = qi & 1
    for g in groups:
        flash_init(g)

    @pl.when(odd == 1)
    def _():
        for g in groups:
            scores(ksel_s, g, 0, q_sel[g], 0)
        for g in groups:
            softmax(g, 0, None)
        for g in groups:
            pv(vsel_s, g, 0, 0)

    for g in groups:
        p_s[g, 1] = jnp.zeros(p_s.shape[2:], BF16)
        alpha_s[g, 1] = jnp.ones(alpha_s.shape[2:], F32)
        scores(ksel_s, g, odd, q_sel[g], 0)

    def stage(j, slot):
        for g in groups:
            scores(ksel_s, g, j + 1, q_sel[g], 1 - slot)
        for g in groups:
            pv(vsel_s, g, jnp.maximum(j - 1, 0), 1 - slot)
        for g in groups:
            softmax(g, slot, None)

    def sel_body(jj, carry):
        j = odd + 2 * jj
        stage(j, 0)
        stage(j + 1, 1)
        return carry

    lax.fori_loop(0, (qi - odd) // 2, sel_body, 0)
    for g in groups:
        pv(vsel_s, g, jnp.maximum(qi - 1, 0), 1)
    for g in groups:
        softmax(g, 0, causal)
    for g in groups:
        pv(vsel_s, g, qi, 0)
    o_sels = [flash_result(g) for g in groups]

    heads = []
    for g in groups:
        for r in range(NSA_REP):
            c = (g * NSA_REP + r) * 3
            sl = slice(r * tile, (r + 1) * tile)
            heads.append(gates[c:c + 1] * o_cmps[g][:, sl] + gates[c + 1:c + 2] * o_sels[g][:, sl]
                         + gates[c + 2:c + 3] * o_wins[g][:, sl])
    y_ref[0] = jnp.concatenate(heads, axis=0).T.astype(y_ref.dtype)


def _overlap_matrix(seq):
    starts = np.arange(seq // CMP_STRIDE) * CMP_STRIDE
    bstart = np.arange(seq // SEL_LEN) * SEL_LEN
    ovl = (starts[None, :] < bstart[:, None] + SEL_LEN) & (starts[None, :] + CMP_LEN > bstart[:, None])
    return jnp.asarray(ovl, dtype=BF16)


def _nsa_attention(bqr, bqn, blog, kcd, vct, ksl, vsl, kwn, vwn):
    batch, seq, _ = bqr.shape
    tile = NSA_TILE
    assert 2 * tile == WIN_LEN
    cols = NSA_REP * tile
    ncmp = seq // CMP_STRIDE
    nblk = seq // SEL_LEN
    qspec = lambda w: pl.BlockSpec((1, tile, w), lambda b, i: (b, i, 0))
    kvspec = pl.BlockSpec((1, seq, LANES), lambda b, i: (b, 0, 0))
    cspec = pl.BlockSpec((1, NSA_KV, ncmp, LANES), lambda b, i: (b, 0, 0, 0))
    k_scratch = pltpu.VMEM((NSA_KV, seq, LANES), BF16)
    v_scratch = pltpu.VMEM((NSA_KV, seq // tile, V_ROWS, tile), BF16)
    return pl.pallas_call(
        functools.partial(_nsa_kernel, tile=tile, seq=seq),
        out_shape=jax.ShapeDtypeStruct((batch, seq, NSA_Q_WIDTH), BF16),
        grid=(batch, seq // tile),
        in_specs=[qspec(NSA_Q_WIDTH), qspec(NSA_Q_WIDTH), qspec(LANES), cspec, cspec,
                  kvspec, kvspec, kvspec, kvspec,
                  pl.BlockSpec((nblk, ncmp), lambda b, i: (0, 0))],
        out_specs=qspec(NSA_Q_WIDTH),
        scratch_shapes=[k_scratch, v_scratch, k_scratch, v_scratch,
                        pltpu.VMEM((NSA_KV, 1, cols), F32), pltpu.VMEM((NSA_KV, V_ROWS, cols), F32),
                        pltpu.VMEM((NSA_KV, 2, 1, cols), F32), pltpu.VMEM((NSA_KV, 2, tile, cols), F32),
                        pltpu.VMEM((NSA_KV, 2, tile, cols), BF16)],
        compiler_params=_params(("arbitrary", "arbitrary")),
        name="nsa_attention",
    )(bqr, bqn, blog, kcd, vct, ksl, vsl, kwn, vwn, _overlap_matrix(seq))


def _merge_ffn_kernel(o0_ref, o1_ref, o2_ref, l0_ref, l1_ref, l2_ref, yb_ref, ga_ref, gb_ref, x_ref,
                      gate_ref, wa_ref, wb_ref, wo_ref, shift2_ref, scale2_ref, gate2_ref, g2_ref,
                      wup_ref, cw_ref, cb_ref, wdn_ref, fin_ref, out_ref, il_s, he_s, act_s, carry_s,
                      *, final):
    i = pl.program_id(1)
    tm = x_ref.shape[1]

    def natural(ref, dil, slot):
        if dil == 1:
            return ref[0].astype(F32)
        nblk = A_GROUP_WIDTH // LANES
        for r in range(dil):
            for j in range(nblk):
                c = r * A_GROUP_WIDTH + j * LANES
                il_s[slot, j, pl.ds(r, tm // dil, stride=dil), :] = ref[0, :, c:c + LANES].astype(F32)
        return jnp.concatenate([il_s[slot, j] for j in range(nblk)], axis=1)

    dils = [dil for _, dil in DIL_GROUPS]
    o0, o1, o2 = (natural(r, dil, n) for n, (r, dil) in enumerate(zip((o0_ref, o1_ref, o2_ref), dils)))
    l0, l1, l2 = (natural(r, dil, 3 + n) for n, (r, dil) in enumerate(zip((l0_ref, l1_ref, l2_ref), dils)))
    m = jnp.maximum(jnp.maximum(l0, l1), l2)
    e0, e1, e2 = jnp.exp(l0 - m), jnp.exp(l1 - m), jnp.exp(l2 - m)
    inv = 1.0 / (e0 + e1 + e2)
    y_a = (e0 * inv) * o0 + (e1 * inv) * o1 + (e2 * inv) * o2
    pa = jnp.dot(y_a.astype(BF16), wa_ref[...], preferred_element_type=F32)
    pb = jnp.dot(yb_ref[0], wb_ref[...], preferred_element_type=F32)
    merged = _sigmoid(ga_ref[0].astype(F32)) * pa + _sigmoid(gb_ref[0].astype(F32)) * pb
    out = jnp.dot(merged.astype(BF16), wo_ref[...], preferred_element_type=F32)
    out_ref[0] = x_ref[0] + gate_ref[...] * out

    @pl.when(i == 0)
    def _():
        carry_s[...] = jnp.zeros(carry_s.shape, F32)

    hx = _norm_mod(out_ref[0], g2_ref[...], scale2_ref[...], shift2_ref[...])
    he_s[0:tm] = hx.astype(BF16)
    he_s[tm:] = carry_s[...].astype(BF16)
    carry_s[...] = hx[tm - HALO:]

    def up(col):
        return jnp.dot(he_s[...], wup_ref[:, col:col + FFN_CHUNK], preferred_element_type=F32)

    def conv(u, col):
        w = cw_ref[:, col:col + FFN_CHUNK]
        return (w[0:1] * pltpu.roll(u, 2, 0)[:tm] + w[1:2] * pltpu.roll(u, 1, 0)[:tm]
                + (w[2:3] * u[:tm] + cb_ref[:, col:col + FFN_CHUNK]))

    chunks = list(range(0, D_FF, FFN_CHUNK))
    pending = [(up(c), up(D_FF + c)) for c in chunks[:FFN_LOOKAHEAD]]
    acc = None
    lo_c = 0
    group = -(-len(chunks) // FFN_DOWN_SPLITS)
    for n, c in enumerate(chunks):
        ug, uv = pending.pop(0)
        if n + FFN_LOOKAHEAD < len(chunks):
            ahead = chunks[n + FFN_LOOKAHEAD]
            pending.append((up(ahead), up(D_FF + ahead)))
        act_s[:, c:c + FFN_CHUNK] = (_silu(conv(ug, c)) * conv(uv, D_FF + c)).astype(BF16)
        if (n + 1) % group == 0 or n + 1 == len(chunks):
            hi_c = c + FFN_CHUNK
            part = jnp.dot(act_s[:, lo_c:hi_c], wdn_ref[lo_c:hi_c, :], preferred_element_type=F32)
            acc = part if acc is None else acc + part
            lo_c = hi_c
    y = out_ref[0] + gate2_ref[...] * acc
    if final:
        ms = jnp.mean(y * y, axis=-1, keepdims=True)
        y = y * lax.rsqrt(ms + RMS_EPS) * fin_ref[...]
    out_ref[0] = y


def _merge_ffn(o_lse, y_b, ga, gb, x, mod_l, w_a, w_b, w_o, norm_g, w_up, conv_w, conv_b, w_down,
               final_g, final):
    batch, seq, d = x.shape
    tm = TAIL_TILE
    row = lambda w, dil=1: pl.BlockSpec((1, tm // dil, dil * w), lambda b, i: (b, i, 0))
    vec = lambda k: pl.BlockSpec((None, None, 1, d), lambda b, i, k=k: (b, k, 0, 0))
    full = lambda a: _resident(a.shape)
    (o0, l0), (o1, l1), (o2, l2) = o_lse
    grp = [row(A_GROUP_WIDTH, dil) for _, dil in DIL_GROUPS]
    norm_g = norm_g.reshape(1, d)
    conv_b = conv_b.reshape(1, -1)
    final_g = final_g.reshape(1, d)
    return pl.pallas_call(
        functools.partial(_merge_ffn_kernel, final=final),
        out_shape=jax.ShapeDtypeStruct(x.shape, F32),
        grid=(batch, seq // tm),
        in_specs=grp + grp + [row(NSA_Q_WIDTH), row(d), row(d), row(d), vec(2),
                              full(w_a), full(w_b), full(w_o), vec(3), vec(4), vec(5), full(norm_g),
                              full(w_up), full(conv_w), full(conv_b), full(w_down), full(final_g)],
        out_specs=row(d),
        scratch_shapes=[pltpu.VMEM((6, A_GROUP_WIDTH // LANES, tm, LANES), F32),
                        pltpu.VMEM((tm + HALO, d), BF16), pltpu.VMEM((tm, D_FF), BF16),
                        pltpu.VMEM((HALO, d), F32)],
        compiler_params=_params(("arbitrary", "arbitrary")),
        name="merge_ffn",
    )(o0, o1, o2, l0, l1, l2, y_b, ga, gb, x, mod_l, w_a, w_b, w_o, mod_l, mod_l, mod_l, norm_g,
      w_up, conv_w, conv_b, w_down, final_g)


def _rope_tables(seq):
    inv = 1.0 / (ROPE_THETA ** (jnp.arange(0, HEAD_DIM, 2, dtype=F32) / HEAD_DIM))
    ang = jnp.arange(seq, dtype=F32)[:, None] * inv[None, :]
    cos, sin = jnp.cos(ang), jnp.sin(ang)
    cos_t = jnp.tile(cos, (1, LANES // (HEAD_DIM // 2)))
    sin_t = jnp.tile(jnp.concatenate([-sin, sin], axis=1), (1, LANES // HEAD_DIM))
    return cos_t, sin_t


def kernel(x, c, norm1_g, norm2_g, final_g, w_mod, b_mod, w_in, cmp_pe_k, cmp_pe_v, cmp_w1_k, cmp_w2_k, cmp_w1_v, cmp_w2_v, w_br_a, w_br_b, w_out, w_up, conv_w, conv_b, w_down):
    batch, seq, d = x.shape
    depth = w_in.shape[0]
    cos_t, sin_t = _rope_tables(seq)
    mod = _modulation(c, w_mod, b_mod).reshape(depth, batch, N_MOD, 1, d)
    n_in = w_in.shape[2]
    main_in = n_in // LANES * LANES
    w_in_tail = jnp.pad(w_in[:, :, main_in:], ((0, 0), (0, 0), (0, N_IN_PACKED - n_in)))
    for layer in range(depth):
        mod_l = mod[layer]
        (qkv0, qkv1, qkv2, bqr, bqn, kc_raw, vc_raw, ksl, vsl, kwn, vwn, blog, ga, gb) = _input_projection(
            x, mod_l, norm1_g[layer], cos_t, sin_t, _layer_weight_bf16(w_in, layer, w_in_tail))
        o_lse = [_banded_attention(qkv, dil) for qkv, (_, dil) in zip((qkv0, qkv1, qkv2), DIL_GROUPS)]

        pk = _pack_compress_weights(cmp_pe_k[layer], cmp_w1_k[layer], cmp_w2_k[layer])
        pv = _pack_compress_weights(cmp_pe_v[layer], cmp_w1_v[layer], cmp_w2_v[layer])
        pep, w1p, w2p = (jnp.stack([a, b]) for a, b in zip(pk, pv))
        kcd, vcd = _compress(kc_raw, vc_raw, pep, w1p, w2p)
        y_b = _nsa_attention(bqr, bqn, blog, kcd, vcd, ksl, vsl, kwn, vwn)

        x = _merge_ffn(o_lse, y_b, ga, gb, x, mod_l, _layer_weight_bf16(w_br_a, layer),
                       _layer_weight_bf16(w_br_b, layer), _layer_weight_bf16(w_out, layer), norm2_g[layer],
                       _layer_weight_bf16(w_up, layer), conv_w[layer], conv_b[layer],
                       _layer_weight_bf16(w_down, layer), final_g, final=(layer == depth - 1))
    return x
```

```python
import functools

import jax
import jax.numpy as jnp
import numpy as np
from jax import lax
from jax.experimental import pallas as pl
from jax.experimental.pallas import tpu as pltpu

F32 = jnp.float32
BF16 = jnp.bfloat16

HEAD_DIM = 64
ROPE_THETA = 10000.0
RMS_EPS = 1e-6
NEG_INF = -1e30
FORCE_SCORE = 1e9
MASK_BIG = float(2.0 ** 100)
LOG2E = 1.4426950408889634
LN2 = 0.6931471805599453

DIL_GROUPS = ((128, 1), (512, 4), (2048, 16))
A_HEADS = 4
A_GROUP_WIDTH = A_HEADS * HEAD_DIM
BAND_BACK = 128
NSA_HEADS = 8
NSA_KV = 2
NSA_REP = NSA_HEADS // NSA_KV
NSA_Q_WIDTH = NSA_HEADS * HEAD_DIM
CMP_LEN = 32
CMP_STRIDE = 16
CMP_HID = 256
SEL_LEN = 64
SEL_TOPK = 16
WIN_LEN = 512
D_FF = 2816
CONV_W = 3
N_MOD = 6

LANES = 128
VMEM_LIMIT = 56 * 1024 * 1024

ROW_TILE = 1024
TAIL_TILE = 512
NSA_TILE = 256
BANDED_UNITS = 8
FFN_CHUNK = 256
FFN_LOOKAHEAD = 2
FFN_DOWN_SPLITS = 1
HALO = 8
V_ROWS = 80


def _sigmoid(x):
    return 0.5 * jnp.tanh(0.5 * x) + 0.5


def _silu(x):
    h = 0.5 * x
    return h + h * jnp.tanh(h)


def _params(semantics):
    return pltpu.CompilerParams(dimension_semantics=semantics, vmem_limit_bytes=VMEM_LIMIT)


def _resident(shape):
    nd = len(shape)
    return pl.BlockSpec(tuple(shape), lambda *_: (0,) * nd, pipeline_mode=pl.Buffered(1))


def _lane_iota(shape):
    return lax.broadcasted_iota(jnp.int32, shape, len(shape) - 1)


def _row_iota(shape):
    return lax.broadcasted_iota(jnp.int32, shape, len(shape) - 2)


def _mod_kernel(c_ref, w_ref, b_ref, o_ref):
    k = pl.program_id(1)
    part = jnp.dot(_silu(c_ref[...]), w_ref[0], preferred_element_type=F32)

    @pl.when(k == 0)
    def _():
        o_ref[0] = part + b_ref[0]

    @pl.when(k > 0)
    def _():
        o_ref[0] += part


def _modulation(c, w_mod, b_mod):
    depth, d, n = w_mod.shape
    batch = c.shape[0]
    tk = 256
    return pl.pallas_call(
        _mod_kernel,
        out_shape=jax.ShapeDtypeStruct((depth, batch, n), F32),
        grid=(depth, d // tk),
        in_specs=[
            pl.BlockSpec((batch, tk), lambda l, k: (0, k)),
            pl.BlockSpec((1, tk, n), lambda l, k: (l, k, 0)),
            pl.BlockSpec((1, 1, n), lambda l, k: (l, 0, 0)),
        ],
        out_specs=pl.BlockSpec((1, batch, n), lambda l, k: (l, 0, 0)),
        compiler_params=_params(("arbitrary", "arbitrary")),
        name="modulation",
    )(c, w_mod, b_mod.reshape(depth, 1, n))


def _norm_mod(x, g, scale, shift):
    ms = jnp.mean(x * x, axis=-1, keepdims=True)
    y = x * lax.rsqrt(ms + RMS_EPS) * g
    return y * (1.0 + scale) + shift


def _rope_chunk(xc, cos, sin_signed, first_half):
    partner = jnp.where(first_half, pltpu.roll(xc, LANES - HEAD_DIM // 2, 1),
                        pltpu.roll(xc, HEAD_DIM // 2, 1))
    return xc * cos + partner * sin_signed


_QKV_A = 3 * A_GROUP_WIDTH
_A_WIDTH = 3 * A_GROUP_WIDTH
_OFF_BQ = 3 * _A_WIDTH
_OFF_KV = _OFF_BQ + NSA_Q_WIDTH
_OFF_TAIL = _OFF_KV + 6 * LANES
_N_BLOG = NSA_HEADS * 3
_TAIL_WIDTH = 17 * LANES
N_IN_PACKED = _OFF_TAIL + _TAIL_WIDTH


def _inproj_kernel(x_ref, shift_ref, scale_ref, g_ref, cos_ref, sin_ref, w_ref, wt_ref,
                   qkv0_ref, qkv1_ref, qkv2_ref, bqr_ref, bqn_ref, kc_ref, vc_ref,
                   ksl_ref, vsl_ref, kwn_ref, vwn_ref, blog_ref, ga_ref, gb_ref, il_s):
    h = _norm_mod(x_ref[0], g_ref[...], scale_ref[...], shift_ref[...]).astype(BF16)
    cos = cos_ref[...]
    sin = sin_ref[...]
    rows = cos.shape[0]
    first_half = (_lane_iota((rows, LANES)) % HEAD_DIM) < (HEAD_DIM // 2)
    scale = HEAD_DIM ** -0.5 * LOG2E

    def proj(off, width):
        return jnp.dot(h, w_ref[:, off:off + width], preferred_element_type=F32)

    def rope(y, mult):
        parts = [_rope_chunk(y[:, c:c + LANES], cos, sin, first_half) * mult
                 for c in range(0, y.shape[1], LANES)]
        return parts[0] if len(parts) == 1 else jnp.concatenate(parts, axis=1)

    def store_strided(out_ref, y, dil, col):
        nblk = y.shape[1] // LANES
        width = out_ref.shape[2] // dil
        for j in range(nblk):
            il_s[j] = y[:, j * LANES:(j + 1) * LANES]
        for r in range(dil):
            for j in range(nblk):
                c = r * width + col + j * LANES
                out_ref[0, :, c:c + LANES] = il_s[j, pl.ds(r, rows // dil, stride=dil), :].astype(out_ref.dtype)

    for g, ((_, dil), out_ref) in enumerate(zip(DIL_GROUPS, (qkv0_ref, qkv1_ref, qkv2_ref))):
        base = g * A_GROUP_WIDTH
        pieces = (rope(proj(base, 256), scale), rope(proj(_A_WIDTH + base, 256), 1.0),
                  proj(2 * _A_WIDTH + base, 256))
        for n, y in enumerate(pieces):
            if dil == 1:
                out_ref[0, :, n * 256:(n + 1) * 256] = y.astype(out_ref.dtype)
            else:
                store_strided(out_ref, y, dil, n * 256)

    bq = proj(_OFF_BQ, NSA_Q_WIDTH)
    bqn_ref[0] = (bq * scale).astype(bqn_ref.dtype)
    bqr_ref[0] = rope(bq, scale).astype(bqr_ref.dtype)

    kv = proj(_OFF_KV, 6 * LANES)
    store_strided(kc_ref, kv[:, 0:128], CMP_STRIDE, 0)
    store_strided(vc_ref, kv[:, 128:256], CMP_STRIDE, 0)
    ksl_ref[0] = rope(kv[:, 256:384], 1.0).astype(ksl_ref.dtype)
    vsl_ref[0] = kv[:, 384:512].astype(vsl_ref.dtype)
    kwn_ref[0] = rope(kv[:, 512:640], 1.0).astype(kwn_ref.dtype)
    vwn_ref[0] = kv[:, 640:768].astype(vwn_ref.dtype)

    tail = jnp.concatenate([proj(_OFF_TAIL, _TAIL_WIDTH - LANES),
                            jnp.dot(h, wt_ref[...], preferred_element_type=F32)], axis=1)
    blog_ref[0] = tail[:, 0:LANES]
    ga_ref[0] = tail[:, _N_BLOG:_N_BLOG + 1024].astype(ga_ref.dtype)
    gb_ref[0] = tail[:, _N_BLOG + 1024:_N_BLOG + 2048].astype(gb_ref.dtype)


def _split_w_in(w):
    main = w.shape[1] // LANES * LANES
    tail = jnp.pad(w[:, main:], ((0, 0), (0, LANES - (w.shape[1] - main))))
    return w[:, :main].astype(BF16), tail.astype(BF16)


def _input_projection(x, mod_l, norm_g, cos_t, sin_t, w_main, w_tail):
    batch, seq, d = x.shape
    tm = ROW_TILE
    row = lambda w, dil=1: pl.BlockSpec((1, tm // dil, dil * w), lambda b, i: (b, i, 0))
    vec = lambda k: pl.BlockSpec((None, None, 1, d), lambda b, i, k=k: (b, k, 0, 0))
    shp = lambda w, dt, dil=1: jax.ShapeDtypeStruct((batch, seq // dil, dil * w), dt)
    dils = [dil for _, dil in DIL_GROUPS]
    return pl.pallas_call(
        _inproj_kernel,
        out_shape=[shp(_QKV_A, BF16, dil) for dil in dils] + [shp(512, BF16)] * 2
        + [shp(128, BF16, CMP_STRIDE)] * 2 + [shp(128, F32)] * 4 + [shp(128, F32)] + [shp(1024, BF16)] * 2,
        grid=(batch, seq // tm),
        in_specs=[
            row(d), vec(0), vec(1),
            pl.BlockSpec((1, d), lambda b, i: (0, 0)),
            pl.BlockSpec((tm, LANES), lambda b, i: (i, 0)),
            pl.BlockSpec((tm, LANES), lambda b, i: (i, 0)),
            _resident(w_main.shape), _resident(w_tail.shape),
        ],
        out_specs=[row(_QKV_A, dil) for dil in dils] + [row(512)] * 2 + [row(128, CMP_STRIDE)] * 2
        + [row(128)] * 5 + [row(1024)] * 2,
        scratch_shapes=[pltpu.VMEM((A_GROUP_WIDTH // LANES, tm, LANES), F32)],
        compiler_params=_params(("arbitrary", "arbitrary")),
        name="input_projection",
    )(x, mod_l, mod_l, norm_g.reshape(1, d), cos_t, sin_t, w_main, w_tail)


def _banded_kernel(qkv_ref, o_ref, lse_ref, *, length, nres):
    blk = 128
    nq = max(1, min(BANDED_UNITS // (2 * nres), length // blk))
    nkeys = 2 * blk if length > blk else blk
    lane = _lane_iota((blk, LANES))
    lo = lane < HEAD_DIM
    qrow = _row_iota((2 * blk, nkeys)) % blk
    kcol = _lane_iota((2 * blk, nkeys))

    def rows_of(i, qb):
        r0 = pl.multiple_of((i * nq + qb) * blk, blk)
        ks = pl.multiple_of(jnp.maximum(r0 - blk, 0), blk) if length > blk else 0
        return r0, ks

    def score(i, qb, res, pair):
        base = res * _QKV_A + pair * LANES
        r0, ks = rows_of(i, qb)
        qp = qkv_ref[0, pl.ds(r0, blk), base:base + LANES]
        k2 = qkv_ref[0, pl.ds(ks, nkeys), base + 256:base + 256 + LANES]
        zero = jnp.zeros_like(qp)
        qs = jnp.concatenate([jnp.where(lo, qp, zero), jnp.where(lo, zero, qp)], axis=0)
        s = lax.dot_general(qs, k2, (((1,), (1,)), ((), ())), preferred_element_type=F32)
        dist = (r0 + qrow) - (ks + kcol)
        return jnp.where((dist >= 0) & (dist <= BAND_BACK), s, NEG_INF)

    def soft(s):
        m = jnp.max(s, axis=-1, keepdims=True)
        e = jnp.exp2(s - m)
        den = jnp.sum(e, axis=-1, keepdims=True)
        return e.astype(BF16), 1.0 / den, m * LN2 + jnp.log(den)

    def emit(i, qb, res, pair, e, inv, lse):
        base = res * _QKV_A + pair * LANES
        r0, ks = rows_of(i, qb)
        v2 = qkv_ref[0, pl.ds(ks, nkeys), base + 512:base + 512 + LANES]
        o = jnp.dot(e, v2, preferred_element_type=F32) * inv
        ob = res * A_GROUP_WIDTH + pair * LANES
        o_ref[0, pl.ds(r0, blk), ob:ob + LANES] = jnp.where(lo, o[:blk], o[blk:]).astype(o_ref.dtype)
        lse_ref[0, pl.ds(r0, blk), ob:ob + LANES] = jnp.where(
            lo, jnp.broadcast_to(lse[:blk], (blk, LANES)), jnp.broadcast_to(lse[blk:], (blk, LANES)))

    units = [(qb, res, pair) for qb in range(nq) for res in range(nres) for pair in range(2)]

    def body(i, carry):
        ss = [score(i, *u) for u in units]
        ps = [soft(s) for s in ss]
        for u, p in zip(units, ps):
            emit(i, *u, *p)
        return carry

    lax.fori_loop(0, length // (blk * nq), body, 0)


def _banded_attention(view, dil):
    batch, length, _ = view.shape
    nres = min(dil, 4)
    return pl.pallas_call(
        functools.partial(_banded_kernel, length=length, nres=nres),
        out_shape=[jax.ShapeDtypeStruct((batch, length, dil * A_GROUP_WIDTH), BF16),
                   jax.ShapeDtypeStruct((batch, length, dil * A_GROUP_WIDTH), F32)],
        grid=(batch, dil // nres),
        in_specs=[pl.BlockSpec((1, length, nres * _QKV_A), lambda b, r: (b, 0, r))],
        out_specs=[pl.BlockSpec((1, length, nres * A_GROUP_WIDTH), lambda b, r: (b, 0, r))] * 2,
        compiler_params=_params(("arbitrary", "arbitrary")),
        name=f"banded_attention_d{dil}",
    )(view)


def _compress_kernel(k_ref, v_ref, pe_ref, w1_ref, w2_ref, kc_ref, vc_ref):
    nchunk = k_ref.shape[1]
    for t, (src, dst) in enumerate(((k_ref, kc_ref), (v_ref, vc_ref))):
        c = src[0].astype(F32)
        top = jnp.dot((c + pe_ref[t, 0]).astype(BF16), w1_ref[t, 0], preferred_element_type=F32)
        bot = jnp.dot((c + pe_ref[t, 1]).astype(BF16), w1_ref[t, 1], preferred_element_type=F32)
        pre = top + pltpu.roll(bot, nchunk - 1, 0)
        for g in range(NSA_KV):
            hid = _silu(pre[:, g * CMP_HID:(g + 1) * CMP_HID]).astype(BF16)
            res = jnp.dot(hid, w2_ref[t], preferred_element_type=F32)
            dst[0, g] = (res if t == 0 else res.T).astype(dst.dtype)


def _pack_compress_weights(pe, w1, w2):
    half = CMP_LEN // 2
    w1r = w1.reshape(2, half, HEAD_DIM, CMP_HID)
    zero = jnp.zeros_like(w1r)
    w1p = jnp.stack([jnp.concatenate([w1r, zero], axis=-1), jnp.concatenate([zero, w1r], axis=-1)], axis=2)
    w1p = w1p.reshape(2, half * LANES, NSA_KV * CMP_HID)
    pep = jnp.tile(pe.reshape(2, half, 1, HEAD_DIM), (1, 1, NSA_KV, 1)).reshape(2, 1, half * LANES)
    w2p = jnp.concatenate([w2, w2], axis=1)
    return pep, w1p.astype(BF16), w2p.astype(BF16)


def _compress(kc_raw, vc_raw, pep, w1p, w2p):
    batch, nchunk, width = kc_raw.shape
    view = lambda a: a
    full = lambda a: _resident(a.shape)
    out = jax.ShapeDtypeStruct((batch, NSA_KV, nchunk, LANES), BF16)
    return pl.pallas_call(
        _compress_kernel,
        out_shape=[out, out],
        grid=(batch,),
        in_specs=[pl.BlockSpec((1, nchunk, width), lambda b: (b, 0, 0))] * 2
        + [full(pep), full(w1p), full(w2p)],
        out_specs=[pl.BlockSpec((1, NSA_KV, nchunk, LANES), lambda b: (b, 0, 0, 0))] * 2,
        compiler_params=_params(("arbitrary",)),
        name="nsa_compress",
    )(view(kc_raw), view(vc_raw), pep, w1p, w2p)


def _nsa_kernel(bqr_ref, bqn_ref, blog_ref, kcd_ref, vct_ref, ksl_ref, vsl_ref, kwn_ref, vwn_ref,
                ovl_ref, y_ref, ksel_s, vsel_s, kwin_s, vwin_s, m_s, acc_s, alpha_s, s_s, p_s, *, tile, seq):
    qi = pl.program_id(1)
    q0 = qi * tile
    cols = NSA_REP * tile
    nblk = seq // SEL_LEN
    ncmp = kcd_ref.shape[2]
    ntile = seq // tile
    nt = (((1,), (1,)), ((), ()))

    @pl.when(qi == 0)
    def _build_kv():
        lane = _lane_iota((seq, LANES))
        lo = lane < HEAD_DIM
        blk_of_row = _row_iota((seq, LANES)) // SEL_LEN
        selmask = jnp.where(blk_of_row == lane - HEAD_DIM, -MASK_BIG, 0.0)
        for src, dst, fill in ((ksl_ref, ksel_s, selmask), (kwn_ref, kwin_s, 0.0)):
            a = src[0]
            dst[0] = jnp.where(lo, a, fill).astype(dst.dtype)
            dst[1] = jnp.where(lo, pltpu.roll(a, HEAD_DIM, 1), fill).astype(dst.dtype)
        ones_rows = jnp.where(_row_iota((V_ROWS - HEAD_DIM, seq)) == 0, 1.0, 0.0)
        for src, dst in ((vsl_ref, vsel_s), (vwn_ref, vwin_s)):
            vt = src[0].T
            for g in range(NSA_KV):
                full = jnp.concatenate([vt[g * HEAD_DIM:(g + 1) * HEAD_DIM], ones_rows], axis=0).astype(dst.dtype)
                for kt in range(ntile):
                    dst[g, kt] = full[:, kt * tile:(kt + 1) * tile]

    lo_t = _lane_iota((tile, LANES)) < HEAD_DIM

    def stack_heads(q_ref, g, extra):
        out = []
        for pair in range(2):
            c = g * 256 + pair * LANES
            qp = q_ref[0, :, c:c + LANES].astype(F32)
            out.append(jnp.where(lo_t, qp, extra))
            out.append(jnp.where(lo_t, pltpu.roll(qp, HEAD_DIM, 1), extra))
        return jnp.concatenate(out, axis=0).astype(BF16)

    qpos = q0 + (_lane_iota((tile, cols)) % tile)
    krow = _row_iota((tile, cols))

    def scores(k_s, g, kt, qs, slot):
        k0 = pl.multiple_of(kt * tile, tile)
        s_s[g, slot] = lax.dot_general(k_s[g, pl.ds(k0, tile), :], qs, nt, preferred_element_type=F32)

    def softmax(g, slot, keep):
        s = s_s[g, slot]
        if keep is not None:
            s = jnp.where(keep, s, NEG_INF)
        m_old = m_s[g]
        m_new = jnp.maximum(m_old, jnp.max(s, axis=0, keepdims=True))
        alpha_s[g, slot] = jnp.exp2(m_old - m_new)
        p_s[g, slot] = jnp.exp2(s - m_new).astype(BF16)
        m_s[g] = m_new

    def pv(v_s, g, kt, slot):
        acc_s[g] = alpha_s[g, slot] * acc_s[g] + jnp.dot(v_s[g, kt], p_s[g, slot], preferred_element_type=F32)

    def flash_init(g):
        m_s[g] = jnp.full(m_s.shape[1:], NEG_INF, F32)
        acc_s[g] = jnp.zeros(acc_s.shape[1:], F32)

    def flash_result(g):
        acc = acc_s[g]
        return acc[0:HEAD_DIM] * (1.0 / acc[HEAD_DIM:HEAD_DIM + 1])

    gates = _sigmoid(blog_ref[0].T)
    zero_t = jnp.zeros((tile, LANES), F32)
    blk = _row_iota((nblk, tile))
    tpos = q0 + _lane_iota((nblk, tile))
    groups = range(NSA_KV)
    causal = qpos >= q0 + krow

    def cmp_scores(g):
        qn = stack_heads(bqn_ref, g, zero_t)
        return lax.dot_general(kcd_ref[0, g], qn, nt, preferred_element_type=F32)

    def cmp_probs(s):
        tq = q0 + (_lane_iota((ncmp, cols)) % tile)
        cvis = (_row_iota((ncmp, cols)) * CMP_STRIDE + (CMP_LEN - 1)) <= tq
        s = jnp.where(cvis, s, NEG_INF)
        m = jnp.max(s, axis=0, keepdims=True)
        e = jnp.where(cvis, jnp.exp2(s - m), 0.0)
        den = jnp.sum(e, axis=0, keepdims=True)
        return e * jnp.where(den > 0.0, 1.0 / den, 0.0)

    def cmp_products(g, p):
        o_cmp = jnp.dot(vct_ref[0, g, 0:HEAD_DIM, :], p.astype(BF16), preferred_element_type=F32)
        psum = p[:, 0:tile] + p[:, tile:2 * tile] + p[:, 2 * tile:3 * tile] + p[:, 3 * tile:4 * tile]
        p_hi = psum.astype(BF16)
        p_lo = (psum - p_hi.astype(F32)).astype(BF16)
        ovl = ovl_ref[...]
        imp = (jnp.dot(ovl, p_hi, preferred_element_type=F32)
               + jnp.dot(ovl, p_lo, preferred_element_type=F32))
        return o_cmp, imp

    def select_queries(g, imp):
        cur = tpos // SEL_LEN
        forced = (blk == 0) | (blk == cur) | (blk == cur - 1)
        imp = jnp.where(forced, FORCE_SCORE, imp)
        imp = jnp.where(blk * SEL_LEN <= tpos, imp, NEG_INF)
        rank = jnp.zeros((nblk, tile), F32)
        for k in range(nblk):
            vk = imp[k:k + 1, :]
            tie = jnp.where(blk > k, 1.0, 0.0)
            rank = rank + jnp.where(vk > imp, 1.0, jnp.where(vk == imp, tie, 0.0))
        notsel = jnp.where(rank >= SEL_TOPK, 1.0, 0.0)
        extra = jnp.concatenate([jnp.zeros((HEAD_DIM, tile), F32), notsel,
                                 jnp.zeros((LANES - HEAD_DIM - nblk, tile), F32)], axis=0).T.astype(BF16)
        return q_win[g] + jnp.concatenate([extra] * NSA_REP, axis=0)

    far, near = qi - 2, qi - 1
    keep_far = (qpos - (far * tile + krow)) <= jnp.where(far >= 0, WIN_LEN - 1, -1)
    keep_near = krow >= jnp.where(near >= 0, 0, tile)
    q_win = [stack_heads(bqr_ref, g, zero_t) for g in groups]
    for g in groups:
        flash_init(g)
        scores(kwin_s, g, jnp.maximum(far, 0), q_win[g], 0)
    for g in groups:
        scores(kwin_s, g, jnp.maximum(near, 0), q_win[g], 1)
    cmp_s = [cmp_scores(g) for g in groups]
    for g in groups:
        softmax(g, 0, keep_far)
    cmp_p = [cmp_probs(s) for s in cmp_s]
    for g in groups:
        scores(kwin_s, g, qi, q_win[g], 0)
    for g in groups:
        pv(vwin_s, g, jnp.maximum(far, 0), 0)
    o_cmps, imps = zip(*[cmp_products(g, cmp_p[g]) for g in groups])
    for g in groups:
        softmax(g, 1, keep_near)
    for g in groups:
        pv(vwin_s, g, jnp.maximum(near, 0), 1)
    q_sel = [select_queries(g, imps[g]) for g in groups]
    for g in groups:
        softmax(g, 0, causal)
    for g in groups:
        pv(vwin_s, g, qi, 0)
    o_wins = [flash_result(g) for g in groups]

    odd = qi & 1
    for g in groups:
        flash_init(g)

    @pl.when(odd == 1)
    def _():
        for g in groups:
            scores(ksel_s, g, 0, q_sel[g], 0)
        for g in groups:
            softmax(g, 0, None)
        for g in groups:
            pv(vsel_s, g, 0, 0)

    for g in groups:
        p_s[g, 1] = jnp.zeros(p_s.shape[2:], BF16)
        alpha_s[g, 1] = jnp.ones(alpha_s.shape[2:], F32)
        scores(ksel_s, g, odd, q_sel[g], 0)

    def stage(j, slot):
        for g in groups:
            scores(ksel_s, g, j + 1, q_sel[g], 1 - slot)
        for g in groups:
            pv(vsel_s, g, jnp.maximum(j - 1, 0), 1 - slot)
        for g in groups:
            softmax(g, slot, None)

    def sel_body(jj, carry):
        j = odd + 2 * jj
        stage(j, 0)
        stage(j + 1, 1)
        return carry

    lax.fori_loop(0, (qi - odd) // 2, sel_body, 0)
    for g in groups:
        pv(vsel_s, g, jnp.maximum(qi - 1, 0), 1)
    for g in groups:
        softmax(g, 0, causal)
    for g in groups:
        pv(vsel_s, g, qi, 0)
    o_sels = [flash_result(g) for g in groups]

    heads = []
    for g in groups:
        for r in range(NSA_REP):
            c = (g * NSA_REP + r) * 3
            sl = slice(r * tile, (r + 1) * tile)
            heads.append(gates[c:c + 1] * o_cmps[g][:, sl] + gates[c + 1:c + 2] * o_sels[g][:, sl]
                         + gates[c + 2:c + 3] * o_wins[g][:, sl])
    y_ref[0] = jnp.concatenate(heads, axis=0).T.astype(y_ref.dtype)


def _overlap_matrix(seq):
    starts = np.arange(seq // CMP_STRIDE) * CMP_STRIDE
    bstart = np.arange(seq // SEL_LEN) * SEL_LEN
    ovl = (starts[None, :] < bstart[:, None] + SEL_LEN) & (starts[None, :] + CMP_LEN > bstart[:, None])
    return jnp.asarray(ovl, dtype=BF16)


def _nsa_attention(bqr, bqn, blog, kcd, vct, ksl, vsl, kwn, vwn):
    batch, seq, _ = bqr.shape
    tile = NSA_TILE
    assert 2 * tile == WIN_LEN
    cols = NSA_REP * tile
    ncmp = seq // CMP_STRIDE
    nblk = seq // SEL_LEN
    qspec = lambda w: pl.BlockSpec((1, tile, w), lambda b, i: (b, i, 0))
    kvspec = pl.BlockSpec((1, seq, LANES), lambda b, i: (b, 0, 0))
    cspec = pl.BlockSpec((1, NSA_KV, ncmp, LANES), lambda b, i: (b, 0, 0, 0))
    k_scratch = pltpu.VMEM((NSA_KV, seq, LANES), BF16)
    v_scratch = pltpu.VMEM((NSA_KV, seq // tile, V_ROWS, tile), BF16)
    return pl.pallas_call(
        functools.partial(_nsa_kernel, tile=tile, seq=seq),
        out_shape=jax.ShapeDtypeStruct((batch, seq, NSA_Q_WIDTH), BF16),
        grid=(batch, seq // tile),
        in_specs=[qspec(NSA_Q_WIDTH), qspec(NSA_Q_WIDTH), qspec(LANES), cspec, cspec,
                  kvspec, kvspec, kvspec, kvspec,
                  pl.BlockSpec((nblk, ncmp), lambda b, i: (0, 0))],
        out_specs=qspec(NSA_Q_WIDTH),
        scratch_shapes=[k_scratch, v_scratch, k_scratch, v_scratch,
                        pltpu.VMEM((NSA_KV, 1, cols), F32), pltpu.VMEM((NSA_KV, V_ROWS, cols), F32),
                        pltpu.VMEM((NSA_KV, 2, 1, cols), F32), pltpu.VMEM((NSA_KV, 2, tile, cols), F32),
                        pltpu.VMEM((NSA_KV, 2, tile, cols), BF16)],
        compiler_params=_params(("arbitrary", "arbitrary")),
        name="nsa_attention",
    )(bqr, bqn, blog, kcd, vct, ksl, vsl, kwn, vwn, _overlap_matrix(seq))


def _merge_ffn_kernel(o0_ref, o1_ref, o2_ref, l0_ref, l1_ref, l2_ref, yb_ref, ga_ref, gb_ref, x_ref,
                      gate_ref, wa_ref, wb_ref, wo_ref, shift2_ref, scale2_ref, gate2_ref, g2_ref,
                      wup_ref, cw_ref, cb_ref, wdn_ref, fin_ref, out_ref, il_s, he_s, act_s, carry_s,
                      *, final):
    i = pl.program_id(1)
    tm = x_ref.shape[1]

    def natural(ref, dil, slot):
        if dil == 1:
            return ref[0].astype(F32)
        nblk = A_GROUP_WIDTH // LANES
        for r in range(dil):
            for j in range(nblk):
                c = r * A_GROUP_WIDTH + j * LANES
                il_s[slot, j, pl.ds(r, tm // dil, stride=dil), :] = ref[0, :, c:c + LANES].astype(F32)
        return jnp.concatenate([il_s[slot, j] for j in range(nblk)], axis=1)

    dils = [dil for _, dil in DIL_GROUPS]
    o0, o1, o2 = (natural(r, dil, n) for n, (r, dil) in enumerate(zip((o0_ref, o1_ref, o2_ref), dils)))
    l0, l1, l2 = (natural(r, dil, 3 + n) for n, (r, dil) in enumerate(zip((l0_ref, l1_ref, l2_ref), dils)))
    m = jnp.maximum(jnp.maximum(l0, l1), l2)
    e0, e1, e2 = jnp.exp(l0 - m), jnp.exp(l1 - m), jnp.exp(l2 - m)
    inv = 1.0 / (e0 + e1 + e2)
    y_a = (e0 * inv) * o0 + (e1 * inv) * o1 + (e2 * inv) * o2
    pa = jnp.dot(y_a.astype(BF16), wa_ref[...], preferred_element_type=F32)
    pb = jnp.dot(yb_ref[0], wb_ref[...], preferred_element_type=F32)
    merged = _sigmoid(ga_ref[0].astype(F32)) * pa + _sigmoid(gb_ref[0].astype(F32)) * pb
    out = jnp.dot(merged.astype(BF16), wo_ref[...], preferred_element_type=F32)
    out_ref[0] = x_ref[0] + gate_ref[...] * out

    @pl.when(i == 0)
    def _():
        carry_s[...] = jnp.zeros(carry_s.shape, F32)

    hx = _norm_mod(out_ref[0], g2_ref[...], scale2_ref[...], shift2_ref[...])
    he_s[0:tm] = hx.astype(BF16)
    he_s[tm:] = carry_s[...].astype(BF16)
    carry_s[...] = hx[tm - HALO:]

    def up(col):
        return jnp.dot(he_s[...], wup_ref[:, col:col + FFN_CHUNK], preferred_element_type=F32)

    def conv(u, col):
        w = cw_ref[:, col:col + FFN_CHUNK]
        return (w[0:1] * pltpu.roll(u, 2, 0)[:tm] + w[1:2] * pltpu.roll(u, 1, 0)[:tm]
                + (w[2:3] * u[:tm] + cb_ref[:, col:col + FFN_CHUNK]))

    chunks = list(range(0, D_FF, FFN_CHUNK))
    pending = [(up(c), up(D_FF + c)) for c in chunks[:FFN_LOOKAHEAD]]
    acc = None
    lo_c = 0
    group = -(-len(chunks) // FFN_DOWN_SPLITS)
    for n, c in enumerate(chunks):
        ug, uv = pending.pop(0)
        if n + FFN_LOOKAHEAD < len(chunks):
            ahead = chunks[n + FFN_LOOKAHEAD]
            pending.append((up(ahead), up(D_FF + ahead)))
        act_s[:, c:c + FFN_CHUNK] = (_silu(conv(ug, c)) * conv(uv, D_FF + c)).astype(BF16)
        if (n + 1) % group == 0 or n + 1 == len(chunks):
            hi_c = c + FFN_CHUNK
            part = jnp.dot(act_s[:, lo_c:hi_c], wdn_ref[lo_c:hi_c, :], preferred_element_type=F32)
            acc = part if acc is None else acc + part
            lo_c = hi_c
    y = out_ref[0] + gate2_ref[...] * acc
    if final:
        ms = jnp.mean(y * y, axis=-1, keepdims=True)
        y = y * lax.rsqrt(ms + RMS_EPS) * fin_ref[...]
    out_ref[0] = y


def _merge_ffn(o_lse, y_b, ga, gb, x, mod_l, w_a, w_b, w_o, norm_g, w_up, conv_w, conv_b, w_down,
               final_g, final):
    batch, seq, d = x.shape
    tm = TAIL_TILE
    row = lambda w, dil=1: pl.BlockSpec((1, tm // dil, dil * w), lambda b, i: (b, i, 0))
    vec = lambda k: pl.BlockSpec((None, None, 1, d), lambda b, i, k=k: (b, k, 0, 0))
    full = lambda a: _resident(a.shape)
    (o0, l0), (o1, l1), (o2, l2) = o_lse
    grp = [row(A_GROUP_WIDTH, dil) for _, dil in DIL_GROUPS]
    norm_g = norm_g.reshape(1, d)
    conv_b = conv_b.reshape(1, -1)
    final_g = final_g.reshape(1, d)
    return pl.pallas_call(
        functools.partial(_merge_ffn_kernel, final=final),
        out_shape=jax.ShapeDtypeStruct(x.shape, F32),
        grid=(batch, seq // tm),
        in_specs=grp + grp + [row(NSA_Q_WIDTH), row(d), row(d), row(d), vec(2),
                              full(w_a), full(w_b), full(w_o), vec(3), vec(4), vec(5), full(norm_g),
                              full(w_up), full(conv_w), full(conv_b), full(w_down), full(final_g)],
        out_specs=row(d),
        scratch_shapes=[pltpu.VMEM((6, A_GROUP_WIDTH // LANES, tm, LANES), F32),
                        pltpu.VMEM((tm + HALO, d), BF16), pltpu.VMEM((tm, D_FF), BF16),
                        pltpu.VMEM((HALO, d), F32)],
        compiler_params=_params(("arbitrary", "arbitrary")),
        name="merge_ffn",
    )(o0, o1, o2, l0, l1, l2, y_b, ga, gb, x, mod_l, w_a, w_b, w_o, mod_l, mod_l, mod_l, norm_g,
      w_up, conv_w, conv_b, w_down, final_g)


def _rope_tables(seq):
    inv = 1.0 / (ROPE_THETA ** (jnp.arange(0, HEAD_DIM, 2, dtype=F32) / HEAD_DIM))
    ang = jnp.arange(seq, dtype=F32)[:, None] * inv[None, :]
    cos, sin = jnp.cos(ang), jnp.sin(ang)
    cos_t = jnp.tile(cos, (1, LANES // (HEAD_DIM // 2)))
    sin_t = jnp.tile(jnp.concatenate([-sin, sin], axis=1), (1, LANES // HEAD_DIM))
    return cos_t, sin_t


def kernel(x, c, norm1_g, norm2_g, final_g, w_mod, b_mod, w_in, cmp_pe_k, cmp_pe_v, cmp_w1_k, cmp_w2_k, cmp_w1_v, cmp_w2_v, w_br_a, w_br_b, w_out, w_up, conv_w, conv_b, w_down):
    batch, seq, d = x.shape
    depth = w_in.shape[0]
    cos_t, sin_t = _rope_tables(seq)
    mod = _modulation(c, w_mod, b_mod).reshape(depth, batch, N_MOD, 1, d)
    for layer in range(depth):
        mod_l = mod[layer]
        (qkv0, qkv1, qkv2, bqr, bqn, kc_raw, vc_raw, ksl, vsl, kwn, vwn, blog, ga, gb) = _input_projection(
            x, mod_l, norm1_g[layer], cos_t, sin_t, *_split_w_in(w_in[layer]))
        o_lse = [_banded_attention(qkv, dil) for qkv, (_, dil) in zip((qkv0, qkv1, qkv2), DIL_GROUPS)]

        pk = _pack_compress_weights(cmp_pe_k[layer], cmp_w1_k[layer], cmp_w2_k[layer])
        pv = _pack_compress_weights(cmp_pe_v[layer], cmp_w1_v[layer], cmp_w2_v[layer])
        pep, w1p, w2p = (jnp.stack([a, b]) for a, b in zip(pk, pv))
        kcd, vcd = _compress(kc_raw, vc_raw, pep, w1p, w2p)
        y_b = _nsa_attention(bqr, bqn, blog, kcd, vcd, ksl, vsl, kwn, vwn)

        x = _merge_ffn(o_lse, y_b, ga, gb, x, mod_l, w_br_a[layer].astype(BF16),
                       w_br_b[layer].astype(BF16), w_out[layer].astype(BF16), norm2_g[layer],
                       w_up[layer].astype(BF16), conv_w[layer], conv_b[layer],
                       w_down[layer].astype(BF16), final_g, final=(layer == depth - 1))
    return x
```
